```python
import jax
import jax.numpy as jnp
from jax import lax
import numpy as np

D_MODEL = 1024
BATCH = 8
SEQ = 8192
DEPTH = 2

EPS = 1e-6
LRU_WIDTH = D_MODEL // 2
LRU_BLOCKS = 8
LRU_BLOCK = LRU_WIDTH // LRU_BLOCKS
LRU_CONV = 4
LRU_C = 8.0
HG_HEADS = 4
HG_DK = 128
HG_DV = 128
HG_WIDTH = HG_HEADS * HG_DV
HG_CHUNK = 64
EVEN_SPLITS = (LRU_WIDTH, 2 * LRU_WIDTH, 2 * LRU_WIDTH + HG_HEADS * HG_DK, 2 * LRU_WIDTH + 2 * HG_HEADS * HG_DK, 2 * LRU_WIDTH + 2 * HG_HEADS * HG_DK + HG_WIDTH)
EVEN_IN = 2 * LRU_WIDTH + 2 * HG_HEADS * HG_DK + 2 * HG_WIDTH
EVEN_OUT = LRU_WIDTH + HG_WIDTH
SGU_WIDTH = D_MODEL
SGU_GROUPS = 8
SGU_GROUP = SGU_WIDTH // SGU_GROUPS
SGU_CHUNK = 128
D_FF = 2816
FFN_CONV = 3
N_EVEN = (DEPTH + 1) // 2
N_ODD = DEPTH // 2

kernel_name = "hybrid_rglru_hgrn2_gmlp_convffn"


def rms_norm(x, g):
    xf = x.astype(jnp.float32)
    xf = xf * lax.rsqrt(jnp.mean(xf * xf, axis=-1, keepdims=True) + EPS)
    return (xf * g.astype(jnp.float32)).astype(x.dtype)


def layer_norm(x, g, b):
    xf = x.astype(jnp.float32)
    xc = xf - jnp.mean(xf, axis=-1, keepdims=True)
    var = jnp.mean(xc * xc, axis=-1, keepdims=True)
    return (xc * lax.rsqrt(var + EPS) * g.astype(jnp.float32) + b.astype(jnp.float32)).astype(x.dtype)


def causal_dwconv(x, w, b):
    k_w = w.shape[0]
    t = x.shape[1]
    xp = jnp.pad(x, ((0, 0), (k_w - 1, 0), (0, 0)))
    out = b
    for k in range(k_w):
        out = out + xp[:, k:k + t] * w[k]
    return out


def rg_lru(x, w_a, b_a, w_x, b_x, lam):
    bsz, t, _ = x.shape
    xf = x.astype(jnp.float32)
    xb = xf.reshape(bsz, t, LRU_BLOCKS, LRU_BLOCK)
    gate_r = jax.nn.sigmoid(jnp.einsum("btni,nij->btnj", xb, w_a.astype(jnp.float32)).reshape(bsz, t, LRU_WIDTH) + b_a.astype(jnp.float32))
    gate_i = jax.nn.sigmoid(jnp.einsum("btni,nij->btnj", xb, w_x.astype(jnp.float32)).reshape(bsz, t, LRU_WIDTH) + b_x.astype(jnp.float32))
    log_a = -LRU_C * gate_r * jax.nn.softplus(-lam.astype(jnp.float32))
    a = jnp.exp(log_a)
    u = jnp.sqrt(-jnp.expm1(2.0 * log_a)) * (gate_i * xf)

    def combine(left, right):
        a_l, h_l = left
        a_r, h_r = right
        return a_l * a_r, a_r * h_l + h_r

    _, h = lax.associative_scan(combine, (a, u), axis=1)
    return h


def hgrn2(q, f_logit, v, g, lb, g_norm):
    bsz, t = q.shape[:2]
    n_c = t // HG_CHUNK
    lbh = lb.astype(jnp.float32).reshape(HG_HEADS, HG_DK)
    f = lbh + (1.0 - lbh) * jax.nn.sigmoid(f_logit.astype(jnp.float32))
    shp_k = (bsz, n_c, HG_CHUNK, HG_HEADS, HG_DK)
    shp_v = (bsz, n_c, HG_CHUNK, HG_HEADS, HG_DV)
    k = (1.0 - f).reshape(shp_k)
    qf = jax.nn.silu(q.astype(jnp.float32)).reshape(shp_k)
    vf = v.astype(jnp.float32).reshape(shp_v)
    b_cum = jnp.cumsum(jnp.log(f).reshape(shp_k), axis=2)
    b_mid = b_cum[:, :, HG_CHUNK // 2 - 1:HG_CHUNK // 2]
    b_last = b_cum[:, :, HG_CHUNK - 1:]
    scores = jnp.einsum("bnthd,bnshd->bnhts", qf * jnp.exp(b_cum - b_mid), k * jnp.exp(b_mid - b_cum))
    causal = jnp.tril(jnp.ones((HG_CHUNK, HG_CHUNK), dtype=bool))
    scores = jnp.where(causal, scores, 0.0)
    o_intra = jnp.einsum("bnhts,bnshv->bnthv", scores, vf)
    q_in = qf * jnp.exp(b_cum)
    kv = jnp.einsum("bnshd,bnshv->bnhdv", k * jnp.exp(b_last - b_cum), vf)
    decay = jnp.exp(b_last[:, :, 0])

    def step(state, xs):
        d_n, kv_n, q_n = xs
        o_n = jnp.einsum("bthd,bhdv->bthv", q_n, state)
        return d_n[..., None] * state + kv_n, o_n

    s0 = jnp.zeros((bsz, HG_HEADS, HG_DK, HG_DV), jnp.float32)
    _, o_inter = lax.scan(step, s0, (jnp.moveaxis(decay, 1, 0), jnp.moveaxis(kv, 1, 0), jnp.moveaxis(q_in, 1, 0)))
    o = (o_intra + jnp.moveaxis(o_inter, 0, 1)).reshape(bsz, t, HG_HEADS, HG_DV)
    o = rms_norm(o, g_norm) * jax.nn.silu(g.astype(jnp.float32))
    return o.reshape(bsz, t, HG_WIDTH)


def even_mixer(h, w_in, conv_w, conv_b, ga_w, ga_b, gx_w, gx_b, lam, lb, hg_norm, w_out):
    bsz, t, _ = h.shape
    z = h @ w_in
    y_gate, x_rec, q, f_logit, v, g = jnp.split(z, list(EVEN_SPLITS), axis=-1)
    x_rec = causal_dwconv(x_rec, conv_w, conv_b)
    out_a = jax.nn.gelu(y_gate.astype(jnp.float32)) * rg_lru(x_rec, ga_w, ga_b, gx_w, gx_b, lam)
    out_b = hgrn2(q.reshape(bsz, t, HG_HEADS, HG_DK), f_logit.reshape(bsz, t, HG_HEADS, HG_DK), v.reshape(bsz, t, HG_HEADS, HG_DV), g.reshape(bsz, t, HG_HEADS, HG_DV), lb, hg_norm)
    return jnp.concatenate([out_a, out_b], axis=-1).astype(h.dtype) @ w_out


def odd_mixer(h, w_in, b_in, ln_g, ln_b, w_s, b_s, w_out):
    bsz, t, _ = h.shape
    n_c = t // SGU_CHUNK
    z = jax.nn.gelu(h @ w_in + b_in)
    u, v = jnp.split(z, 2, axis=-1)
    v = layer_norm(v, ln_g, ln_b).reshape(bsz, n_c, SGU_CHUNK, SGU_GROUPS, SGU_GROUP)
    w_causal = jnp.where(jnp.tril(jnp.ones((SGU_CHUNK, SGU_CHUNK), dtype=bool)), w_s, 0.0)
    sv = jnp.einsum("gts,bnsgc->bntgc", w_causal, v) + b_s.T[:, :, None]
    return (u * sv.reshape(bsz, t, SGU_WIDTH)) @ w_out


def conv_ffn(h, w_up, conv_w, conv_b, w_down):
    gate, up = jnp.split(h @ w_up, 2, axis=-1)
    gate = causal_dwconv(gate, conv_w, conv_b)
    return (jax.nn.silu(gate) * up) @ w_down


def setup_inputs(seed: int = 0) -> dict:
    key = jax.random.key(seed)
    ks = jax.random.split(key, 26)
    f32 = jnp.float32

    def nrm(k, shape, scale):
        return jax.random.normal(k, shape, f32) * scale

    def gain(k, shape):
        return 1.0 + 0.02 * jax.random.normal(k, shape, f32)

    a_c = jax.random.uniform(ks[11], (N_EVEN, LRU_WIDTH), f32, 0.9, 0.999)
    s = a_c ** (1.0 / LRU_C)
    lam = jnp.log(s) - jnp.log1p(-s)
    return {
        "x": nrm(ks[0], (BATCH, SEQ, D_MODEL), 1.0),
        "norm_mix": gain(ks[1], (DEPTH, D_MODEL)),
        "norm_ffn": gain(ks[2], (DEPTH, D_MODEL)),
        "norm_final": gain(ks[3], (D_MODEL,)),
        "ev_w_in": nrm(ks[4], (N_EVEN, D_MODEL, EVEN_IN), D_MODEL ** -0.5),
        "ev_conv_w": nrm(ks[5], (N_EVEN, LRU_CONV, LRU_WIDTH), LRU_CONV ** -0.5),
        "ev_conv_b": nrm(ks[6], (N_EVEN, LRU_WIDTH), 0.02),
        "ev_gate_a_w": nrm(ks[7], (N_EVEN, LRU_BLOCKS, LRU_BLOCK, LRU_BLOCK), LRU_BLOCK ** -0.5),
        "ev_gate_a_b": nrm(ks[8], (N_EVEN, LRU_WIDTH), 0.02),
        "ev_gate_x_w": nrm(ks[9], (N_EVEN, LRU_BLOCKS, LRU_BLOCK, LRU_BLOCK), LRU_BLOCK ** -0.5),
        "ev_gate_x_b": nrm(ks[10], (N_EVEN, LRU_WIDTH), 0.02),
        "ev_lru_lambda": lam,
        "hg_lb_logits": nrm(ks[12], (DEPTH + 1, HG_HEADS * HG_DK), 0.1),
        "ev_hg_norm": gain(ks[13], (N_EVEN, HG_DV)),
        "ev_w_out": nrm(ks[14], (N_EVEN, EVEN_OUT, D_MODEL), EVEN_OUT ** -0.5),
        "od_w_in": nrm(ks[15], (N_ODD, D_MODEL, 2 * SGU_WIDTH), D_MODEL ** -0.5),
        "od_b_in": nrm(ks[16], (N_ODD, 2 * SGU_WIDTH), 0.02),
        "od_ln_g": gain(ks[17], (N_ODD, SGU_WIDTH)),
        "od_ln_b": nrm(ks[18], (N_ODD, SGU_WIDTH), 0.02),
        "od_w_s": nrm(ks[19], (N_ODD, SGU_GROUPS, SGU_CHUNK, SGU_CHUNK), 0.5 * SGU_CHUNK ** -0.5),
        "od_b_s": gain(ks[20], (N_ODD, SGU_GROUPS, SGU_CHUNK)),
        "od_w_out": nrm(ks[21], (N_ODD, SGU_WIDTH, D_MODEL), SGU_WIDTH ** -0.5),
        "ffn_w_up": nrm(ks[22], (DEPTH, D_MODEL, 2 * D_FF), D_MODEL ** -0.5),
        "ffn_conv_w": nrm(ks[23], (DEPTH, FFN_CONV, D_FF), FFN_CONV ** -0.5),
        "ffn_conv_b": nrm(ks[24], (DEPTH, D_FF), 0.02),
        "ffn_w_down": nrm(ks[25], (DEPTH, D_FF, D_MODEL), D_FF ** -0.5),
    }


def reference(x, norm_mix, norm_ffn, norm_final, ev_w_in, ev_conv_w, ev_conv_b, ev_gate_a_w, ev_gate_a_b, ev_gate_x_w, ev_gate_x_b, ev_lru_lambda, hg_lb_logits, ev_hg_norm, ev_w_out, od_w_in, od_b_in, od_ln_g, od_ln_b, od_w_s, od_b_s, od_w_out, ffn_w_up, ffn_conv_w, ffn_conv_b, ffn_w_down):
    lower_bounds = jnp.cumsum(jax.nn.softmax(hg_lb_logits.astype(jnp.float32), axis=0), axis=0)
    h = x
    for layer in range(DEPTH):
        hn = rms_norm(h, norm_mix[layer])
        if layer % 2 == 0:
            e = layer // 2
            mix = even_mixer(hn, ev_w_in[e], ev_conv_w[e], ev_conv_b[e], ev_gate_a_w[e], ev_gate_a_b[e], ev_gate_x_w[e], ev_gate_x_b[e], ev_lru_lambda[e], lower_bounds[layer], ev_hg_norm[e], ev_w_out[e])
        else:
            o = layer // 2
            mix = odd_mixer(hn, od_w_in[o], od_b_in[o], od_ln_g[o], od_ln_b[o], od_w_s[o], od_b_s[o], od_w_out[o])
        h = h + mix.astype(h.dtype)
        h = h + conv_ffn(rms_norm(h, norm_ffn[layer]), ffn_w_up[layer], ffn_conv_w[layer], ffn_conv_b[layer], ffn_w_down[layer]).astype(h.dtype)
    return rms_norm(h, norm_final)
```

```python
import functools

import jax
import jax.numpy as jnp
from jax import lax
from jax.experimental import pallas as pl
from jax.experimental.pallas import tpu as pltpu

F32 = jnp.float32
BF16 = jnp.bfloat16

EPS = 1e-6
LRU_C = 8.0
LRU_BLOCKS = 8
HG_HEADS = 4
HG_CHUNK = 64
SGU_GROUPS = 8
SGU_CHUNK = 128

MXU_EDGE = 256
SUBLANES = 8
VMEM_LIMIT_BYTES = 60 * 1024 * 1024

TIME_TILE = 512


def _rms(x, g):
    return x * lax.rsqrt(jnp.mean(x * x, axis=-1, keepdims=True) + EPS) * g


def _dot(a, b):
    return jnp.dot(a, b, preferred_element_type=F32)


def _dot_nt(a, b):
    return lax.dot_general(a, b, (((1,), (1,)), ((), ())), preferred_element_type=F32)


def _dot_tn(a, b):
    return lax.dot_general(a, b, (((0,), (0,)), ((), ())), preferred_element_type=F32)


def _shift_rows(x, tail, s, row):
    out = pltpu.roll(x, s, 0)
    for i in range(s):
        out = jnp.where(row == i, tail[SUBLANES - s + i:SUBLANES - s + i + 1], out)
    return out


def _causal_conv(x, tail, w, b, row):
    k_w = w.shape[0]
    out = b + w[k_w - 1:k_w] * x
    for s in range(1, k_w):
        out = out + w[k_w - 1 - s:k_w - s] * _shift_rows(x, tail, s, row)
    return out


def _ffn_kernel(h_ref, g_ref, wup_ref, cw_ref, cb_ref, wdn_ref, gfin_ref, o_ref, act_ref, tail_ref,
                *, n_chunks, cw, final):
    @pl.when(pl.program_id(1) == 0)
    def _():
        tail_ref[...] = jnp.zeros_like(tail_ref)

    h = h_ref[0]
    tt = h.shape[0]
    hn = _rms(h, g_ref[...]).astype(BF16)
    row = lax.broadcasted_iota(jnp.int32, (tt, cw), 0)
    for j in range(n_chunks):
        cols = slice(j * cw, (j + 1) * cw)
        gu = _dot(hn, wup_ref[j])
        gate = gu[:, :cw]
        up = gu[:, cw:]
        c = _causal_conv(gate, tail_ref[:, cols], cw_ref[:, cols], cb_ref[:, cols], row)
        tail_ref[:, cols] = gate[tt - SUBLANES:, :]
        act_ref[:, cols] = (c * jax.nn.sigmoid(c) * up).astype(BF16)
    out = h + _dot(act_ref[...], wdn_ref[...])
    if final:
        out = _rms(out, gfin_ref[...])
    o_ref[0] = out


def _const_spec(shape):
    nd = len(shape)
    return pl.BlockSpec(shape, lambda b, t: (0,) * nd, pipeline_mode=pl.Buffered(1))


def _ffn(h, g, w_up, conv_w, conv_b, w_down, g_final, *, final, tt):
    bsz, seq, d = h.shape
    d_ff = w_down.shape[0]
    cw = MXU_EDGE
    n_chunks = d_ff // cw
    assert n_chunks * cw == d_ff and seq % tt == 0
    wg = w_up[:, :d_ff].reshape(d, n_chunks, cw)
    wu = w_up[:, d_ff:].reshape(d, n_chunks, cw)
    wup = jnp.transpose(jnp.concatenate([wg, wu], axis=-1), (1, 0, 2)).astype(BF16)
    kern = functools.partial(_ffn_kernel, n_chunks=n_chunks, cw=cw, final=final)
    return pl.pallas_call(
        kern,
        grid=(bsz, seq // tt),
        in_specs=[
            pl.BlockSpec((1, tt, d), lambda b, t: (b, t, 0)),
            _const_spec((1, d)),
            _const_spec((n_chunks, d, 2 * cw)),
            _const_spec(conv_w.shape),
            _const_spec((1, d_ff)),
            _const_spec((d_ff, d)),
            _const_spec((1, d)),
        ],
        out_specs=pl.BlockSpec((1, tt, d), lambda b, t: (b, t, 0)),
        out_shape=jax.ShapeDtypeStruct(h.shape, h.dtype),
        scratch_shapes=[
            pltpu.VMEM((tt, d_ff), BF16),
            pltpu.VMEM((SUBLANES, d_ff), F32),
        ],
        compiler_params=pltpu.CompilerParams(
            dimension_semantics=("parallel", "arbitrary"), vmem_limit_bytes=VMEM_LIMIT_BYTES),
        name="conv_ffn",
    )(h, g.reshape(1, d), wup, conv_w, conv_b.reshape(1, d_ff), w_down.astype(BF16), g_final.reshape(1, d))


def _odd_kernel(h_ref, g_ref, win_ref, bin_ref, lng_ref, lnb_ref, ws_ref, bias_ref, wout_ref, o_ref,
                bd_ref, gated_ref, *, width, groups, chunk):
    gw = width // groups
    blk = 2 * chunk

    @pl.when(pl.program_id(1) == 0)
    def _():
        r = lax.broadcasted_iota(jnp.int32, (chunk, chunk), 0)
        c = lax.broadcasted_iota(jnp.int32, (chunk, chunk), 1)
        bd_ref[...] = jnp.zeros_like(bd_ref)
        for g in range(groups):
            wc = jnp.where(r >= c, ws_ref[g], 0.0).astype(BF16)
            bd_ref[g, 0:chunk, 0:chunk] = wc
            bd_ref[g, chunk:blk, chunk:blk] = wc

    h = h_ref[0]
    tt = h.shape[0]
    hn = _rms(h, g_ref[...]).astype(BF16)
    z = jax.nn.gelu(_dot(hn, win_ref[...]) + bin_ref[...], approximate=True)
    u = z[:, :width]
    v = z[:, width:]
    vc = v - jnp.mean(v, axis=-1, keepdims=True)
    var = jnp.mean(vc * vc, axis=-1, keepdims=True)
    vn = (vc * lax.rsqrt(var + EPS) * lng_ref[...] + lnb_ref[...]).astype(BF16)
    for r in range(tt // blk):
        rows = slice(r * blk, (r + 1) * blk)
        for g in range(groups):
            cols = slice(g * gw, (g + 1) * gw)
            sv = _dot(bd_ref[g], vn[rows, cols]) + bias_ref[:, cols]
            gated_ref[rows, cols] = (u[rows, cols] * sv).astype(BF16)
    o_ref[0] = h + _dot(gated_ref[...], wout_ref[...])


def _odd_mixer(h, g, w_in, b_in, ln_g, ln_b, w_s, b_s, w_out, *, tt):
    bsz, seq, d = h.shape
    width = w_out.shape[0]
    groups, chunk = w_s.shape[0], w_s.shape[1]
    gw = width // groups
    blk = 2 * chunk
    assert blk == MXU_EDGE and tt % blk == 0 and seq % tt == 0
    bias = jnp.tile(jnp.repeat(b_s.T, gw, axis=1), (2, 1))
    kern = functools.partial(_odd_kernel, width=width, groups=groups, chunk=chunk)
    return pl.pallas_call(
        kern,
        grid=(bsz, seq // tt),
        in_specs=[
            pl.BlockSpec((1, tt, d), lambda b, t: (b, t, 0)),
            _const_spec((1, d)),
            _const_spec((d, 2 * width)),
            _const_spec((1, 2 * width)),
            _const_spec((1, width)),
            _const_spec((1, width)),
            _const_spec(w_s.shape),
            _const_spec((blk, width)),
            _const_spec((width, d)),
        ],
        out_specs=pl.BlockSpec((1, tt, d), lambda b, t: (b, t, 0)),
        out_shape=jax.ShapeDtypeStruct(h.shape, h.dtype),
        scratch_shapes=[
            pltpu.VMEM((groups, blk, blk), BF16),
            pltpu.VMEM((tt, width), BF16),
        ],
        compiler_params=pltpu.CompilerParams(
            dimension_semantics=("parallel", "arbitrary"), vmem_limit_bytes=VMEM_LIMIT_BYTES),
        name="sgu_mixer",
    )(h, g.reshape(1, d), w_in.astype(BF16), b_in.reshape(1, -1), ln_g.reshape(1, -1), ln_b.reshape(1, -1),
      w_s, bias, w_out.astype(BF16))


def _linear_scan(a, u):
    tt = a.shape[0]
    row = lax.broadcasted_iota(jnp.int32, a.shape, 0)
    s = 1
    while s < tt:
        m = row >= s
        u = a * jnp.where(m, pltpu.roll(u, s, 0), 0.0) + u
        a = a * jnp.where(m, pltpu.roll(a, s, 0), 1.0)
        s *= 2
    return a, u


def _even_kernel(h_ref, g_ref, win_ref, cw_ref, cb_ref, wg_ref, ba_ref, bx_ref, lam_ref, lbl_ref, gn_ref,
                 wout_ref, o_ref, mix_ref, st_ref, hprev_ref, xtail_ref, *, layer, lw, hw):
    dk = hw // HG_HEADS
    blk = MXU_EDGE
    n_c = blk // HG_CHUNK

    @pl.when(pl.program_id(1) == 0)
    def _():
        st_ref[...] = jnp.zeros_like(st_ref)
        hprev_ref[...] = jnp.zeros_like(hprev_ref)
        xtail_ref[...] = jnp.zeros_like(xtail_ref)

    h = h_ref[0]
    tt = h.shape[0]
    hn = _rms(h, g_ref[...]).astype(BF16)

    def proj(i, w):
        return _dot(hn, win_ref[:, i:i + w])

    y_gate = proj(0, lw)
    x_rec = proj(lw, lw)
    row = lax.broadcasted_iota(jnp.int32, (tt, lw), 0)
    xc = _causal_conv(x_rec, xtail_ref[...], cw_ref[...], cb_ref[...], row)
    xtail_ref[...] = x_rec[tt - SUBLANES:, :]
    xcb = xc.astype(BF16)
    n_gb = lw // blk
    pre = [_dot(xcb[:, i * blk:(i + 1) * blk], wg_ref[i]) for i in range(n_gb)]
    r_pre = jnp.concatenate([p[:, :blk] for p in pre], axis=1)
    i_pre = jnp.concatenate([p[:, blk:] for p in pre], axis=1)
    gate_r = jax.nn.sigmoid(r_pre + ba_ref[...])
    gate_i = jax.nn.sigmoid(i_pre + bx_ref[...])
    lam = lam_ref[...]
    softplus_neg_lam = jnp.maximum(-lam, 0.0) + jnp.log1p(jnp.exp(-jnp.abs(lam)))
    log_a = (-LRU_C * softplus_neg_lam) * gate_r
    a = jnp.exp(log_a)
    mult = jnp.sqrt(-jnp.tanh(log_a) * (1.0 + a * a))
    a_cum, h_loc = _linear_scan(a, mult * (gate_i * xc))
    h_lru = h_loc + a_cum * hprev_ref[0:1, :]
    hprev_ref[...] = jnp.broadcast_to(h_lru[tt - 1:tt, :], hprev_ref.shape)
    mix_ref[:, 0:lw] = (jax.nn.gelu(y_gate, approximate=True) * h_lru).astype(BF16)

    lbl = lbl_ref[...]
    e = jnp.exp(lbl - jnp.max(lbl, axis=0, keepdims=True))
    sm = e / jnp.sum(e, axis=0, keepdims=True)
    lb = jnp.sum(sm[0:layer + 1], axis=0, keepdims=True)

    q_all = proj(2 * lw, hw)
    f_all = proj(2 * lw + hw, hw)
    v_all = proj(2 * lw + 2 * hw, hw)
    g_all = proj(2 * lw + 3 * hw, hw)

    r_i = lax.broadcasted_iota(jnp.int32, (blk, blk), 0)
    c_i = lax.broadcasted_iota(jnp.int32, (blk, blk), 1)
    tri_mask = (r_i >= c_i) & ((r_i // HG_CHUNK) == (c_i // HG_CHUNK))
    tri = jnp.where(tri_mask, 1.0, 0.0).astype(BF16)
    gn = gn_ref[...]

    for rb in range(tt // blk):
        rows = slice(rb * blk, (rb + 1) * blk)
        f = lb + (1.0 - lb) * jax.nn.sigmoid(f_all[rows])
        k = 1.0 - f
        logf = jnp.log(f)
        hi = logf.astype(BF16)
        lo = (logf - hi.astype(F32)).astype(BF16)
        bcum = _dot(tri, hi) + _dot(tri, lo)
        qf = q_all[rows]
        qf = qf * jax.nn.sigmoid(qf)
        qs, ks, qin, kd, dec = [], [], [], [], []
        for c in range(n_c):
            cr = slice(c * HG_CHUNK, (c + 1) * HG_CHUNK)
            bc = bcum[cr]
            bm = bc[HG_CHUNK // 2 - 1:HG_CHUNK // 2]
            bl = bc[HG_CHUNK - 1:HG_CHUNK]
            qs.append(qf[cr] * jnp.exp(bc - bm))
            ks.append(k[cr] * jnp.exp(bm - bc))
            qin.append((qf[cr] * jnp.exp(bc)).astype(BF16))
            kd.append((k[cr] * jnp.exp(bl - bc)).astype(BF16))
            dec.append(jnp.exp(bl))
        qs = jnp.concatenate(qs, axis=0).astype(BF16)
        ks = jnp.concatenate(ks, axis=0).astype(BF16)
        vb = v_all[rows].astype(BF16)
        gg = g_all[rows]
        for hd in range(HG_HEADS):
            cs = slice(hd * dk, (hd + 1) * dk)
            sc = jnp.where(tri_mask, _dot_nt(qs[:, cs], ks[:, cs]), 0.0).astype(BF16)
            o = _dot(sc, vb[:, cs])
            st = st_ref[hd]
            o_inter = []
            for c in range(n_c):
                cr = slice(c * HG_CHUNK, (c + 1) * HG_CHUNK)
                o_inter.append(_dot_nt(qin[c][:, cs], st.astype(BF16)))
                st = st * dec[c][:, cs] + _dot_tn(vb[cr, cs], kd[c][:, cs])
            st_ref[hd] = st
            o = o + jnp.concatenate(o_inter, axis=0)
            o = _rms(o, gn)
            gh = gg[:, cs]
            mix_ref[rows, lw + hd * dk:lw + (hd + 1) * dk] = (o * (gh * jax.nn.sigmoid(gh))).astype(BF16)

    o_ref[0] = h + _dot(mix_ref[...], wout_ref[...])


def _block_diag(w):
    n, bi, bj = w.shape
    eye = jnp.eye(n, dtype=w.dtype)
    return (w[:, :, None, :] * eye[:, None, :, None]).reshape(n * bi, n * bj)


def _even_mixer(h, g, w_in, conv_w, conv_b, ga_w, ga_b, gx_w, gx_b, lam, lb_logits, hg_norm, w_out, *, layer, tt):
    bsz, seq, d = h.shape
    lw = conv_w.shape[1]
    hw = lb_logits.shape[1]
    dk = hw // HG_HEADS
    blk = MXU_EDGE
    assert lw % blk == 0 and tt % blk == 0 and seq % tt == 0 and hg_norm.shape[0] == dk
    assert w_in.shape[1] == 2 * lw + 4 * hw and w_out.shape[0] == lw + hw
    n_gb = lw // blk
    bda = _block_diag(ga_w)
    bdx = _block_diag(gx_w)
    wg = jnp.stack([
        jnp.concatenate([bda[i * blk:(i + 1) * blk, i * blk:(i + 1) * blk],
                         bdx[i * blk:(i + 1) * blk, i * blk:(i + 1) * blk]], axis=1)
        for i in range(n_gb)]).astype(BF16)
    kern = functools.partial(_even_kernel, layer=layer, lw=lw, hw=hw)
    return pl.pallas_call(
        kern,
        grid=(bsz, seq // tt),
        in_specs=[
            pl.BlockSpec((1, tt, d), lambda b, t: (b, t, 0)),
            _const_spec((1, d)),
            _const_spec(w_in.shape),
            _const_spec(conv_w.shape),
            _const_spec((1, lw)),
            _const_spec(wg.shape),
            _const_spec((1, lw)),
            _const_spec((1, lw)),
            _const_spec((1, lw)),
            _const_spec(lb_logits.shape),
            _const_spec((1, dk)),
            _const_spec(w_out.shape),
        ],
        out_specs=pl.BlockSpec((1, tt, d), lambda b, t: (b, t, 0)),
        out_shape=jax.ShapeDtypeStruct(h.shape, h.dtype),
        scratch_shapes=[
            pltpu.VMEM((tt, lw + hw), BF16),
            pltpu.VMEM((HG_HEADS, dk, dk), F32),
            pltpu.VMEM((SUBLANES, lw), F32),
            pltpu.VMEM((SUBLANES, lw), F32),
        ],
        compiler_params=pltpu.CompilerParams(
            dimension_semantics=("parallel", "arbitrary"), vmem_limit_bytes=VMEM_LIMIT_BYTES),
        name="lru_hgrn_mixer",
    )(h, g.reshape(1, d), w_in.astype(BF16), conv_w, conv_b.reshape(1, lw), wg, ga_b.reshape(1, lw),
      gx_b.reshape(1, lw), lam.reshape(1, lw), lb_logits, hg_norm.reshape(1, dk), w_out.astype(BF16))


def kernel(x, norm_mix, norm_ffn, norm_final, ev_w_in, ev_conv_w, ev_conv_b, ev_gate_a_w, ev_gate_a_b, ev_gate_x_w, ev_gate_x_b, ev_lru_lambda, hg_lb_logits, ev_hg_norm, ev_w_out, od_w_in, od_b_in, od_ln_g, od_ln_b, od_w_s, od_b_s, od_w_out, ffn_w_up, ffn_conv_w, ffn_conv_b, ffn_w_down):
    depth = norm_mix.shape[0]
    tt = min(TIME_TILE, x.shape[1])
    h = x
    for layer in range(depth):
        if layer % 2 == 0:
            e = layer // 2
            h = _even_mixer(h, norm_mix[layer], ev_w_in[e], ev_conv_w[e], ev_conv_b[e], ev_gate_a_w[e],
                            ev_gate_a_b[e], ev_gate_x_w[e], ev_gate_x_b[e], ev_lru_lambda[e], hg_lb_logits,
                            ev_hg_norm[e], ev_w_out[e], layer=layer, tt=tt)
        else:
            o = layer // 2
            h = _odd_mixer(h, norm_mix[layer], od_w_in[o], od_b_in[o], od_ln_g[o], od_ln_b[o], od_w_s[o],
                           od_b_s[o], od_w_out[o], tt=tt)
        h = _ffn(h, norm_ffn[layer], ffn_w_up[layer], ffn_conv_w[layer], ffn_conv_b[layer], ffn_w_down[layer],
                 norm_final, final=(layer == depth - 1), tt=tt)
    return h
```

```python
import functools

import jax
import jax.numpy as jnp
from jax import lax
from jax.experimental import pallas as pl
from jax.experimental.pallas import tpu as pltpu

F32 = jnp.float32
BF16 = jnp.bfloat16

EPS = 1e-6
LRU_C = 8.0
HG_HEADS = 4
HG_CHUNK = 64

MXU_EDGE = 256
SUBLANES = 8
VMEM_LIMIT_BYTES = 60 * 1024 * 1024
TINY = 1.1754944e-38

TIME_TILE = 512


def _rms(x, g):
    return x * lax.rsqrt(jnp.mean(x * x, axis=-1, keepdims=True) + EPS) * g


def _dot(a, b):
    return jnp.dot(a, b, preferred_element_type=F32)


def _dot_nt(a, b):
    return lax.dot_general(a, b, (((1,), (1,)), ((), ())), preferred_element_type=F32)


def _dot_tn(a, b):
    return lax.dot_general(a, b, (((0,), (0,)), ((), ())), preferred_element_type=F32)


def _silu(x):
    return x * jax.nn.sigmoid(x)


def _causal_conv(x, tail, w, b):
    k_w = w.shape[0]
    out = b + w[k_w - 1:k_w] * x
    xx = jnp.concatenate([tail, x], axis=0)
    for s in range(1, k_w):
        out = out + w[k_w - 1 - s:k_w - s] * pltpu.roll(xx, s, 0)[SUBLANES:, :]
    return out


def _col_blocks(w, width):
    k, n = w.shape
    return jnp.transpose(w.reshape(k, n // width, width), (1, 0, 2))


def _ffn_kernel(h_ref, g_ref, wup_ref, cw_ref, cb_ref, wdn_ref, gfin_ref, o_ref, act_ref, tail_ref,
                *, n_chunks, cw, final):
    @pl.when(pl.program_id(1) == 0)
    def _():
        tail_ref[...] = jnp.zeros_like(tail_ref)

    h = h_ref[0]
    tt = h.shape[0]
    hn = _rms(h, g_ref[...]).astype(BF16)
    for j in range(n_chunks):
        cols = slice(j * cw, (j + 1) * cw)
        gu = _dot(hn, wup_ref[j])
        gate = gu[:, :cw]
        up = gu[:, cw:]
        c = _causal_conv(gate, tail_ref[:, cols], cw_ref[:, cols], cb_ref[:, cols])
        tail_ref[:, cols] = gate[tt - SUBLANES:, :]
        act_ref[:, cols] = (_silu(c) * up).astype(BF16)
    out = h + _dot(act_ref[...], wdn_ref[...])
    if final:
        out = _rms(out, gfin_ref[...])
    o_ref[0] = out


def _const_spec(shape):
    nd = len(shape)
    return pl.BlockSpec(shape, lambda b, t: (0,) * nd, pipeline_mode=pl.Buffered(1))


def _ffn(h, g, w_up, conv_w, conv_b, w_down, g_final, *, final, tt):
    bsz, seq, d = h.shape
    d_ff = w_down.shape[0]
    cw = MXU_EDGE
    n_chunks = d_ff // cw
    assert n_chunks * cw == d_ff and seq % tt == 0
    wg = w_up[:, :d_ff].reshape(d, n_chunks, cw)
    wu = w_up[:, d_ff:].reshape(d, n_chunks, cw)
    wup = jnp.transpose(jnp.concatenate([wg, wu], axis=-1), (1, 0, 2)).astype(BF16)
    kern = functools.partial(_ffn_kernel, n_chunks=n_chunks, cw=cw, final=final)
    return pl.pallas_call(
        kern,
        grid=(bsz, seq // tt),
        in_specs=[
            pl.BlockSpec((1, tt, d), lambda b, t: (b, t, 0)),
            _const_spec((1, d)),
            _const_spec((n_chunks, d, 2 * cw)),
            _const_spec(conv_w.shape),
            _const_spec((1, d_ff)),
            _const_spec((d_ff, d)),
            _const_spec((1, d)),
        ],
        out_specs=pl.BlockSpec((1, tt, d), lambda b, t: (b, t, 0)),
        out_shape=jax.ShapeDtypeStruct(h.shape, h.dtype),
        scratch_shapes=[
            pltpu.VMEM((tt, d_ff), BF16),
            pltpu.VMEM((SUBLANES, d_ff), F32),
        ],
        compiler_params=pltpu.CompilerParams(
            dimension_semantics=("parallel", "arbitrary"), vmem_limit_bytes=VMEM_LIMIT_BYTES),
        name="conv_ffn",
    )(h, g.reshape(1, d), wup, conv_w, conv_b.reshape(1, d_ff), w_down.astype(BF16), g_final.reshape(1, d))


def _odd_kernel(h_ref, g_ref, win_ref, bin_ref, lng_ref, lnb_ref, ws_ref, bias_ref, wout_ref, o_ref,
                bd_ref, gated_ref, *, width, groups, chunk):
    gw = width // groups
    blk = 2 * chunk

    @pl.when(pl.program_id(1) == 0)
    def _():
        r = lax.broadcasted_iota(jnp.int32, (chunk, chunk), 0)
        c = lax.broadcasted_iota(jnp.int32, (chunk, chunk), 1)
        bd_ref[...] = jnp.zeros_like(bd_ref)
        for g in range(groups):
            wc = jnp.where(r >= c, ws_ref[g], 0.0).astype(BF16)
            bd_ref[g, 0:chunk, 0:chunk] = wc
            bd_ref[g, chunk:blk, chunk:blk] = wc

    h = h_ref[0]
    tt = h.shape[0]
    hn = _rms(h, g_ref[...]).astype(BF16)
    z = jax.nn.gelu(_dot(hn, win_ref[...]) + bin_ref[...], approximate=True)
    u = z[:, :width]
    v = z[:, width:]
    vc = v - jnp.mean(v, axis=-1, keepdims=True)
    var = jnp.mean(vc * vc, axis=-1, keepdims=True)
    vn = (vc * lax.rsqrt(var + EPS) * lng_ref[...] + lnb_ref[...]).astype(BF16)
    for r in range(tt // blk):
        rows = slice(r * blk, (r + 1) * blk)
        for g in range(groups):
            cols = slice(g * gw, (g + 1) * gw)
            sv = _dot(bd_ref[g], vn[rows, cols]) + bias_ref[:, cols]
            gated_ref[rows, cols] = (u[rows, cols] * sv).astype(BF16)
    o_ref[0] = h + _dot(gated_ref[...], wout_ref[...])


def _odd_mixer(h, g, w_in, b_in, ln_g, ln_b, w_s, b_s, w_out, *, tt):
    bsz, seq, d = h.shape
    width = w_out.shape[0]
    groups, chunk = w_s.shape[0], w_s.shape[1]
    gw = width // groups
    blk = 2 * chunk
    assert blk == MXU_EDGE and tt % blk == 0 and seq % tt == 0
    bias = jnp.tile(jnp.repeat(b_s.T, gw, axis=1), (2, 1))
    kern = functools.partial(_odd_kernel, width=width, groups=groups, chunk=chunk)
    return pl.pallas_call(
        kern,
        grid=(bsz, seq // tt),
        in_specs=[
            pl.BlockSpec((1, tt, d), lambda b, t: (b, t, 0)),
            _const_spec((1, d)),
            _const_spec((d, 2 * width)),
            _const_spec((1, 2 * width)),
            _const_spec((1, width)),
            _const_spec((1, width)),
            _const_spec(w_s.shape),
            _const_spec((blk, width)),
            _const_spec((width, d)),
        ],
        out_specs=pl.BlockSpec((1, tt, d), lambda b, t: (b, t, 0)),
        out_shape=jax.ShapeDtypeStruct(h.shape, h.dtype),
        scratch_shapes=[
            pltpu.VMEM((groups, blk, blk), BF16),
            pltpu.VMEM((tt, width), BF16),
        ],
        compiler_params=pltpu.CompilerParams(
            dimension_semantics=("parallel", "arbitrary"), vmem_limit_bytes=VMEM_LIMIT_BYTES),
        name="sgu_mixer",
    )(h, g.reshape(1, d), w_in.astype(BF16), b_in.reshape(1, -1), ln_g.reshape(1, -1), ln_b.reshape(1, -1),
      w_s, bias, w_out.astype(BF16))


def _linear_scan(a, u, carry):
    n, c = a.shape
    groups = n // SUBLANES
    a3 = a.reshape(groups, SUBLANES, c)
    u3 = u.reshape(groups, SUBLANES, c)
    sub = lax.broadcasted_iota(jnp.int32, a3.shape, 1)
    s = 1
    while s < SUBLANES:
        m = sub >= s
        u3 = a3 * jnp.where(m, pltpu.roll(u3, s, 1), 0.0) + u3
        a3 = a3 * jnp.where(m, pltpu.roll(a3, s, 1), 1.0)
        s *= 2
    out = []
    for i in range(groups):
        hi = u3[i] + a3[i] * carry
        carry = hi[SUBLANES - 1:SUBLANES]
        out.append(hi)
    return jnp.concatenate(out, axis=0), carry


def _even_kernel(x_ref, xnext_ref, g_ref, win_ref, cw_ref, cb_ref, wg_ref, ba_ref, bx_ref, lam_ref, lbl_ref, gn_ref,
                 wout_ref, o_ref, za_ref, zb_ref, st_ref, hprev_ref, xtail_ref, *, layer, lw, hw, tiles_per_row):
    dk = hw // HG_HEADS
    blk = MXU_EDGE
    n_c = blk // HG_CHUNK
    n_gb = lw // blk
    n_proj, _, pw = win_ref.shape
    n_out = wout_ref.shape[0]
    ow = wout_ref.shape[2]
    assert x_ref.shape[0] == 2 * blk
    z_refs = (za_ref, zb_ref)
    row_blocks = (slice(0, blk), slice(blk, 2 * blk))

    s = pl.program_id(0)

    def projection(src, dst_ref):
        cache = []

        def piece(j):
            if not cache:
                cache.append(_rms(src(), g_ref[...]).astype(BF16))
            dst_ref[:, j * pw:(j + 1) * pw] = _dot(cache[0], win_ref[j])

        return [functools.partial(piece, j) for j in range(n_proj)]

    @pl.when(s == 0)
    def _():
        for p in projection(lambda: x_ref[0:blk, :], za_ref):
            p()

    @pl.when(s % tiles_per_row == 0)
    def _():
        st_ref[...] = jnp.zeros_like(st_ref)
        hprev_ref[...] = jnp.zeros_like(hprev_ref)
        xtail_ref[...] = jnp.zeros_like(xtail_ref)

    lam = lam_ref[...]
    softplus_neg_lam = jnp.maximum(-lam, 0.0) + jnp.log1p(jnp.exp(-jnp.abs(lam)))
    log_a_scale = -LRU_C * softplus_neg_lam

    lbl = lbl_ref[...]
    e = jnp.exp(lbl - jnp.max(lbl, axis=0, keepdims=True))
    sm = e / jnp.sum(e, axis=0, keepdims=True)
    lb = jnp.sum(sm[0:layer + 1], axis=0, keepdims=True)

    r_i = lax.broadcasted_iota(jnp.int32, (blk, blk), 0)
    c_i = lax.broadcasted_iota(jnp.int32, (blk, blk), 1)
    tri_mask = (r_i >= c_i) & ((r_i // HG_CHUNK) == (c_i // HG_CHUNK))
    tri = jnp.where(tri_mask, 1.0, 0.0).astype(BF16)
    gn = gn_ref[...]

    carry = {"x_tail": xtail_ref[...], "h": hprev_ref[0:1, :],
             "st": [st_ref[hd] for hd in range(HG_HEADS)]}
    ctx = ({}, {})

    def lru_conv(b):
        x_rec = z_refs[b][:, lw:2 * lw]
        ctx[b]["xc"] = _causal_conv(x_rec, carry["x_tail"], cw_ref[...], cb_ref[...])
        carry["x_tail"] = x_rec[blk - SUBLANES:, :]

    def lru_gates(b):
        xc = ctx[b]["xc"]
        xcb = xc.astype(BF16)
        pre = [_dot(xcb[:, i * blk:(i + 1) * blk], wg_ref[i]) for i in range(n_gb)]
        r_pre = jnp.concatenate([p[:, :blk] for p in pre], axis=1)
        i_pre = jnp.concatenate([p[:, blk:] for p in pre], axis=1)
        gate_r = jax.nn.sigmoid(r_pre + ba_ref[...])
        gate_i = jax.nn.sigmoid(i_pre + bx_ref[...])
        log_a = log_a_scale * gate_r
        a = jnp.exp(log_a)
        y = -jnp.tanh(log_a) * (1.0 + a * a)
        mult = y * lax.rsqrt(jnp.maximum(y, TINY))
        ctx[b]["a"] = a
        ctx[b]["u"] = mult * (gate_i * xc)

    def lru_scan(b):
        ctx[b]["h_lru"], carry["h"] = _linear_scan(ctx[b]["a"], ctx[b]["u"], carry["h"])

    def lru_out(b):
        ctx[b]["out_a"] = (jax.nn.gelu(z_refs[b][:, 0:lw], approximate=True) * ctx[b]["h_lru"]).astype(BF16)

    def hg_decay(b):
        f = lb + (1.0 - lb) * jax.nn.sigmoid(z_refs[b][:, 2 * lw + hw:2 * lw + 2 * hw])
        logf = jnp.log(f)
        hi = logf.astype(BF16)
        lo = (logf - hi.astype(F32)).astype(BF16)
        ctx[b]["k"] = 1.0 - f
        ctx[b]["bcum"] = _dot(tri, hi) + _dot(tri, lo)
        ctx[b]["qf"] = _silu(z_refs[b][:, 2 * lw:2 * lw + hw])

    def hg_factors(b):
        bcum, qf, k = ctx[b]["bcum"], ctx[b]["qf"], ctx[b]["k"]
        qs, ks, qin, kd, dec = [], [], [], [], []
        for c in range(n_c):
            cr = slice(c * HG_CHUNK, (c + 1) * HG_CHUNK)
            bc = bcum[cr]
            bm = bc[HG_CHUNK // 2 - 1:HG_CHUNK // 2]
            bl = bc[HG_CHUNK - 1:HG_CHUNK]
            e_c = jnp.exp(bc - bm)
            qs_c = qf[cr] * e_c
            ks_c = k[cr] * (1.0 / e_c)
            qs.append(qs_c)
            ks.append(ks_c)
            qin.append((qs_c * jnp.exp(bm)).astype(BF16))
            kd.append((ks_c * jnp.exp(bl - bm)).astype(BF16))
            dec.append(jnp.exp(bl))
        ctx[b].update(qs=jnp.concatenate(qs, axis=0).astype(BF16), ks=jnp.concatenate(ks, axis=0).astype(BF16),
                      qin=qin, kd=kd, dec=dec,
                      vb=z_refs[b][:, 2 * lw + 2 * hw:2 * lw + 3 * hw].astype(BF16),
                      gg=_silu(z_refs[b][:, 2 * lw + 3 * hw:2 * lw + 4 * hw]), out_b=[])

    def hg_head(b, hd):
        c_ = ctx[b]
        cs = slice(hd * dk, (hd + 1) * dk)
        vb = c_["vb"]
        sc = _dot_nt(c_["qs"][:, cs], c_["ks"][:, cs])
        kv = [_dot_tn(vb[c * HG_CHUNK:(c + 1) * HG_CHUNK, cs], c_["kd"][c][:, cs]) for c in range(n_c)]
        st = carry["st"][hd]
        st_in = []
        for c in range(n_c):
            st_in.append(st.astype(BF16))
            st = st * c_["dec"][c][:, cs] + kv[c]
        carry["st"][hd] = st
        o = _dot(jnp.where(tri_mask, sc, 0.0).astype(BF16), vb[:, cs])
        o_inter = [_dot_nt(c_["qin"][c][:, cs], st_in[c]) for c in range(n_c)]
        o = _rms(o + jnp.concatenate(o_inter, axis=0), gn)
        c_["out_b"].append((o * c_["gg"][:, cs]).astype(BF16))

    def out_proj(b, j):
        if "mix" not in ctx[b]:
            ctx[b]["mix"] = jnp.concatenate([ctx[b]["out_a"]] + ctx[b]["out_b"], axis=1)
        rows = row_blocks[b]
        cols = slice(j * ow, (j + 1) * ow)
        o_ref[rows, cols] = x_ref[rows, cols] + _dot(ctx[b]["mix"], wout_ref[j])

    P = functools.partial
    proj_b = projection(lambda: x_ref[blk:2 * blk, :], zb_ref)
    proj_next = projection(lambda: xnext_ref[...], za_ref)
    heads0 = [P(hg_head, 0, hd) for hd in range(HG_HEADS)]
    heads1 = [P(hg_head, 1, hd) for hd in range(HG_HEADS)]
    outs0 = [P(out_proj, 0, j) for j in range(n_out)]
    outs1 = [P(out_proj, 1, j) for j in range(n_out)]
    schedule = [
        P(lru_conv, 0), proj_b[0], P(lru_gates, 0), proj_b[1], P(lru_scan, 0), proj_b[2], P(lru_out, 0), proj_b[3],
        P(hg_decay, 0), proj_b[4], P(hg_factors, 0), proj_b[5],
        P(lru_conv, 1), heads0[0], heads0[1], P(lru_gates, 1), heads0[2], heads0[3],
        P(lru_scan, 1), outs0[0], P(lru_out, 1), outs0[1],
        P(hg_decay, 1), proj_next[0], P(hg_factors, 1), proj_next[1],
        heads1[0], proj_next[2], heads1[1], proj_next[3], heads1[2], proj_next[4], heads1[3], proj_next[5],
        outs1[0], outs1[1],
    ]
    assert n_proj == 6 and n_out == 2
    for stage in schedule:
        stage()

    xtail_ref[...] = carry["x_tail"]
    hprev_ref[...] = jnp.broadcast_to(carry["h"], hprev_ref.shape)
    for hd in range(HG_HEADS):
        st_ref[hd] = carry["st"][hd]


def _block_diag(w):
    n, bi, bj = w.shape
    eye = jnp.eye(n, dtype=w.dtype)
    return (w[:, :, None, :] * eye[:, None, :, None]).reshape(n * bi, n * bj)


def _even_mixer(h, g, w_in, conv_w, conv_b, ga_w, ga_b, gx_w, gx_b, lam, lb_logits, hg_norm, w_out, *, layer):
    bsz, seq, d = h.shape
    lw = conv_w.shape[1]
    hw = lb_logits.shape[1]
    dk = hw // HG_HEADS
    blk = MXU_EDGE
    tt = 2 * blk
    assert lw % blk == 0 and seq % tt == 0 and hg_norm.shape[0] == dk
    assert w_in.shape[1] == 2 * lw + 4 * hw and w_out.shape[0] == lw + hw and lw == hw
    n_gb = lw // blk
    bda = _block_diag(ga_w)
    bdx = _block_diag(gx_w)
    wg = jnp.stack([
        jnp.concatenate([bda[i * blk:(i + 1) * blk, i * blk:(i + 1) * blk],
                         bdx[i * blk:(i + 1) * blk, i * blk:(i + 1) * blk]], axis=1)
        for i in range(n_gb)]).astype(BF16)
    win = _col_blocks(w_in, lw).astype(BF16)
    wout = _col_blocks(w_out, 2 * blk).astype(BF16)
    n_tiles = bsz * seq // tt
    x2 = h.reshape(bsz * seq, d)
    kern = functools.partial(_even_kernel, layer=layer, lw=lw, hw=hw, tiles_per_row=seq // tt)

    def cspec(shape):
        nd = len(shape)
        return pl.BlockSpec(shape, lambda s: (0,) * nd, pipeline_mode=pl.Buffered(1))

    out = pl.pallas_call(
        kern,
        grid=(n_tiles,),
        in_specs=[
            pl.BlockSpec((tt, d), lambda s: (s, 0)),
            pl.BlockSpec((blk, d), lambda s: (jnp.minimum(2 * s + 2, 2 * n_tiles - 1), 0)),
            cspec((1, d)),
            cspec(win.shape),
            cspec(conv_w.shape),
            cspec((1, lw)),
            cspec(wg.shape),
            cspec((1, lw)),
            cspec((1, lw)),
            cspec((1, lw)),
            cspec(lb_logits.shape),
            cspec((1, dk)),
            cspec(wout.shape),
        ],
        out_specs=pl.BlockSpec((tt, d), lambda s: (s, 0)),
        out_shape=jax.ShapeDtypeStruct(x2.shape, x2.dtype),
        scratch_shapes=[
            pltpu.VMEM((blk, w_in.shape[1]), F32),
            pltpu.VMEM((blk, w_in.shape[1]), F32),
            pltpu.VMEM((HG_HEADS, dk, dk), F32),
            pltpu.VMEM((SUBLANES, lw), F32),
            pltpu.VMEM((SUBLANES, lw), F32),
        ],
        compiler_params=pltpu.CompilerParams(
            dimension_semantics=("arbitrary",), vmem_limit_bytes=VMEM_LIMIT_BYTES),
        name="lru_hgrn_mixer",
    )(x2, x2, g.reshape(1, d), win, conv_w, conv_b.reshape(1, lw), wg, ga_b.reshape(1, lw),
      gx_b.reshape(1, lw), lam.reshape(1, lw), lb_logits, hg_norm.reshape(1, dk), wout)
    return out.reshape(bsz, seq, d)


def kernel(x, norm_mix, norm_ffn, norm_final, ev_w_in, ev_conv_w, ev_conv_b, ev_gate_a_w, ev_gate_a_b, ev_gate_x_w, ev_gate_x_b, ev_lru_lambda, hg_lb_logits, ev_hg_norm, ev_w_out, od_w_in, od_b_in, od_ln_g, od_ln_b, od_w_s, od_b_s, od_w_out, ffn_w_up, ffn_conv_w, ffn_conv_b, ffn_w_down):
    depth = norm_mix.shape[0]
    tt = min(TIME_TILE, x.shape[1])
    h = x
    for layer in range(depth):
        if layer % 2 == 0:
            e = layer // 2
            h = _even_mixer(h, norm_mix[layer], ev_w_in[e], ev_conv_w[e], ev_conv_b[e], ev_gate_a_w[e],
                            ev_gate_a_b[e], ev_gate_x_w[e], ev_gate_x_b[e], ev_lru_lambda[e], hg_lb_logits,
                            ev_hg_norm[e], ev_w_out[e], layer=layer)
        else:
            o = layer // 2
            h = _odd_mixer(h, norm_mix[layer], od_w_in[o], od_b_in[o], od_ln_g[o], od_ln_b[o], od_w_s[o],
                           od_b_s[o], od_w_out[o], tt=tt)
        h = _ffn(h, norm_ffn[layer], ffn_w_up[layer], ffn_conv_w[layer], ffn_conv_b[layer], ffn_w_down[layer],
                 norm_final, final=(layer == depth - 1), tt=tt)
    return h
```

```python
import functools

import jax
import jax.numpy as jnp
from jax import lax
from jax.experimental import pallas as pl
from jax.experimental.pallas import tpu as pltpu

F32 = jnp.float32
BF16 = jnp.bfloat16

EPS = 1e-6
LRU_C = 8.0
HG_HEADS = 4
HG_CHUNK = 64

MXU_EDGE = 256
SUBLANES = 8
VMEM_LIMIT_BYTES = 60 * 1024 * 1024
TINY = 1.1754944e-38

TIME_TILE = 1024


def _rms(x, g):
    return x * lax.rsqrt(jnp.mean(x * x, axis=-1, keepdims=True) + EPS) * g


def _dot(a, b):
    return jnp.dot(a, b, preferred_element_type=F32)


def _dot_nt(a, b):
    return lax.dot_general(a, b, (((1,), (1,)), ((), ())), preferred_element_type=F32)


def _dot_tn(a, b):
    return lax.dot_general(a, b, (((0,), (0,)), ((), ())), preferred_element_type=F32)


def _silu(x):
    return x * jax.nn.sigmoid(x)


def _causal_conv(x, tail, w, b):
    k_w = w.shape[0]
    out = b + w[k_w - 1:k_w] * x
    xx = jnp.concatenate([tail, x], axis=0)
    for s in range(1, k_w):
        out = out + w[k_w - 1 - s:k_w - s] * pltpu.roll(xx, s, 0)[SUBLANES:, :]
    return out


def _col_blocks(w, width):
    k, n = w.shape
    return jnp.transpose(w.reshape(k, n // width, width), (1, 0, 2))


def _ffn_kernel(h_ref, g_ref, wup_ref, cw_ref, cb_ref, wdn_ref, gfin_ref, o_ref, act_ref, tail_ref,
                *, n_chunks, cw, final):
    @pl.when(pl.program_id(1) == 0)
    def _():
        tail_ref[...] = jnp.zeros_like(tail_ref)

    h = h_ref[0]
    tt = h.shape[0]
    hn = _rms(h, g_ref[...]).astype(BF16)
    for j in range(n_chunks):
        cols = slice(j * cw, (j + 1) * cw)
        gu = _dot(hn, wup_ref[j])
        gate = gu[:, :cw]
        up = gu[:, cw:]
        c = _causal_conv(gate, tail_ref[:, cols], cw_ref[:, cols], cb_ref[:, cols])
        tail_ref[:, cols] = gate[tt - SUBLANES:, :]
        act_ref[:, cols] = (_silu(c) * up).astype(BF16)
    out = h + _dot(act_ref[...], wdn_ref[...])
    if final:
        out = _rms(out, gfin_ref[...])
    o_ref[0] = out


def _const_spec(shape):
    nd = len(shape)
    return pl.BlockSpec(shape, lambda b, t: (0,) * nd, pipeline_mode=pl.Buffered(1))


def _ffn(h, g, w_up, conv_w, conv_b, w_down, g_final, *, final, tt):
    bsz, seq, d = h.shape
    d_ff = w_down.shape[0]
    cw = MXU_EDGE
    n_chunks = d_ff // cw
    assert n_chunks * cw == d_ff and seq % tt == 0
    wg = w_up[:, :d_ff].reshape(d, n_chunks, cw)
    wu = w_up[:, d_ff:].reshape(d, n_chunks, cw)
    wup = jnp.transpose(jnp.concatenate([wg, wu], axis=-1), (1, 0, 2)).astype(BF16)
    kern = functools.partial(_ffn_kernel, n_chunks=n_chunks, cw=cw, final=final)
    return pl.pallas_call(
        kern,
        grid=(bsz, seq // tt),
        in_specs=[
            pl.BlockSpec((1, tt, d), lambda b, t: (b, t, 0)),
            _const_spec((1, d)),
            _const_spec((n_chunks, d, 2 * cw)),
            _const_spec(conv_w.shape),
            _const_spec((1, d_ff)),
            _const_spec((d_ff, d)),
            _const_spec((1, d)),
        ],
        out_specs=pl.BlockSpec((1, tt, d), lambda b, t: (b, t, 0)),
        out_shape=jax.ShapeDtypeStruct(h.shape, h.dtype),
        scratch_shapes=[
            pltpu.VMEM((tt, d_ff), BF16),
            pltpu.VMEM((SUBLANES, d_ff), F32),
        ],
        compiler_params=pltpu.CompilerParams(
            dimension_semantics=("parallel", "arbitrary"), vmem_limit_bytes=VMEM_LIMIT_BYTES),
        name="conv_ffn",
    )(h, g.reshape(1, d), wup, conv_w, conv_b.reshape(1, d_ff), w_down.astype(BF16), g_final.reshape(1, d))


def _odd_kernel(h_ref, g_ref, win_ref, bin_ref, lng_ref, lnb_ref, ws_ref, bias_ref, wout_ref, o_ref,
                bd_ref, *, width, groups, chunk):
    gw = width // groups
    blk = 2 * chunk
    n_proj, _, pw = win_ref.shape
    half = n_proj // 2

    @pl.when(pl.program_id(1) == 0)
    def _():
        r = lax.broadcasted_iota(jnp.int32, (chunk, chunk), 0)
        c = lax.broadcasted_iota(jnp.int32, (chunk, chunk), 1)
        bd_ref[...] = jnp.zeros_like(bd_ref)
        for g in range(groups):
            wc = jnp.where(r >= c, ws_ref[g], 0.0).astype(BF16)
            bd_ref[g, 0:chunk, 0:chunk] = wc
            bd_ref[g, chunk:blk, chunk:blk] = wc

    n_blk = h_ref.shape[1] // blk
    ctx = [dict() for _ in range(n_blk)]

    def proj(b):
        rows = slice(b * blk, (b + 1) * blk)
        hn = _rms(h_ref[0, rows, :], g_ref[...]).astype(BF16)
        ctx[b]["z"] = [_dot(hn, win_ref[j]) for j in range(n_proj)]

    def act(b):
        zs = [jax.nn.gelu(ctx[b]["z"][j] + bin_ref[:, j * pw:(j + 1) * pw], approximate=True)
              for j in range(n_proj)]
        ctx[b]["u"] = jnp.concatenate(zs[:half], axis=1)
        v = jnp.concatenate(zs[half:], axis=1)
        vc = v - jnp.mean(v, axis=-1, keepdims=True)
        var = jnp.mean(vc * vc, axis=-1, keepdims=True)
        ctx[b]["vn"] = (vc * lax.rsqrt(var + EPS) * lng_ref[...] + lnb_ref[...]).astype(BF16)

    def spatial_out(b):
        rows = slice(b * blk, (b + 1) * blk)
        gated = []
        for g in range(groups):
            cols = slice(g * gw, (g + 1) * gw)
            sv = _dot(bd_ref[g], ctx[b]["vn"][:, cols]) + bias_ref[:, cols]
            gated.append((ctx[b]["u"][:, cols] * sv).astype(BF16))
        o_ref[0, rows, :] = h_ref[0, rows, :] + _dot(jnp.concatenate(gated, axis=1), wout_ref[...])

    proj(0)
    for b in range(n_blk):
        act(b)
        if b + 1 < n_blk:
            proj(b + 1)
        if b >= 1:
            spatial_out(b - 1)
    spatial_out(n_blk - 1)


def _odd_mixer(h, g, w_in, b_in, ln_g, ln_b, w_s, b_s, w_out, *, tt):
    bsz, seq, d = h.shape
    width = w_out.shape[0]
    groups, chunk = w_s.shape[0], w_s.shape[1]
    gw = width // groups
    blk = 2 * chunk
    assert blk == MXU_EDGE and tt % blk == 0 and seq % tt == 0
    bias = jnp.tile(jnp.repeat(b_s.T, gw, axis=1), (2, 1))
    win = _col_blocks(w_in, 2 * blk).astype(BF16)
    kern = functools.partial(_odd_kernel, width=width, groups=groups, chunk=chunk)
    return pl.pallas_call(
        kern,
        grid=(bsz, seq // tt),
        in_specs=[
            pl.BlockSpec((1, tt, d), lambda b, t: (b, t, 0)),
            _const_spec((1, d)),
            _const_spec(win.shape),
            _const_spec((1, 2 * width)),
            _const_spec((1, width)),
            _const_spec((1, width)),
            _const_spec(w_s.shape),
            _const_spec((blk, width)),
            _const_spec((width, d)),
        ],
        out_specs=pl.BlockSpec((1, tt, d), lambda b, t: (b, t, 0)),
        out_shape=jax.ShapeDtypeStruct(h.shape, h.dtype),
        scratch_shapes=[
            pltpu.VMEM((groups, blk, blk), BF16),
        ],
        compiler_params=pltpu.CompilerParams(
            dimension_semantics=("parallel", "arbitrary"), vmem_limit_bytes=VMEM_LIMIT_BYTES),
        name="sgu_mixer",
    )(h, g.reshape(1, d), win, b_in.reshape(1, -1), ln_g.reshape(1, -1), ln_b.reshape(1, -1),
      w_s, bias, w_out.astype(BF16))


def _linear_scan(a, u, carry):
    n, c = a.shape
    groups = n // SUBLANES
    a3 = a.reshape(groups, SUBLANES, c)
    u3 = u.reshape(groups, SUBLANES, c)
    sub = lax.broadcasted_iota(jnp.int32, a3.shape, 1)
    s = 1
    while s < SUBLANES:
        m = sub >= s
        u3 = a3 * jnp.where(m, pltpu.roll(u3, s, 1), 0.0) + u3
        a3 = a3 * jnp.where(m, pltpu.roll(a3, s, 1), 1.0)
        s *= 2
    out = []
    for i in range(groups):
        hi = u3[i] + a3[i] * carry
        carry = hi[SUBLANES - 1:SUBLANES]
        out.append(hi)
    return jnp.concatenate(out, axis=0), carry


def _even_kernel(x_ref, xnext_ref, g_ref, win_ref, cw_ref, cb_ref, wg_ref, ba_ref, bx_ref, lam_ref, lbl_ref, gn_ref,
                 wout_ref, o_ref, za_ref, zb_ref, st_ref, hprev_ref, xtail_ref, *, layer, lw, hw, tiles_per_row):
    dk = hw // HG_HEADS
    blk = MXU_EDGE
    n_c = blk // HG_CHUNK
    n_gb = lw // blk
    n_proj, _, pw = win_ref.shape
    n_out = wout_ref.shape[0]
    ow = wout_ref.shape[2]
    assert x_ref.shape[0] == 2 * blk
    z_refs = (za_ref, zb_ref)
    row_blocks = (slice(0, blk), slice(blk, 2 * blk))

    s = pl.program_id(0)

    def projection(src, dst_ref):
        cache = []

        def piece(j):
            if not cache:
                cache.append(_rms(src(), g_ref[...]).astype(BF16))
            dst_ref[:, j * pw:(j + 1) * pw] = _dot(cache[0], win_ref[j])

        return [functools.partial(piece, j) for j in range(n_proj)]

    @pl.when(s == 0)
    def _():
        for p in projection(lambda: x_ref[0:blk, :], za_ref):
            p()

    @pl.when(s % tiles_per_row == 0)
    def _():
        st_ref[...] = jnp.zeros_like(st_ref)
        hprev_ref[...] = jnp.zeros_like(hprev_ref)
        xtail_ref[...] = jnp.zeros_like(xtail_ref)

    lam = lam_ref[...]
    softplus_neg_lam = jnp.maximum(-lam, 0.0) + jnp.log1p(jnp.exp(-jnp.abs(lam)))
    log_a_scale = -LRU_C * softplus_neg_lam

    lbl = lbl_ref[...]
    e = jnp.exp(lbl - jnp.max(lbl, axis=0, keepdims=True))
    sm = e / jnp.sum(e, axis=0, keepdims=True)
    lb = jnp.sum(sm[0:layer + 1], axis=0, keepdims=True)

    r_i = lax.broadcasted_iota(jnp.int32, (blk, blk), 0)
    c_i = lax.broadcasted_iota(jnp.int32, (blk, blk), 1)
    tri_mask = (r_i >= c_i) & ((r_i // HG_CHUNK) == (c_i // HG_CHUNK))
    tri = jnp.where(tri_mask, 1.0, 0.0).astype(BF16)
    gn = gn_ref[...]

    carry = {"x_tail": xtail_ref[...], "h": hprev_ref[0:1, :],
             "st": [st_ref[hd] for hd in range(HG_HEADS)]}
    ctx = ({}, {})

    def lru_conv(b):
        x_rec = z_refs[b][:, lw:2 * lw]
        ctx[b]["xc"] = _causal_conv(x_rec, carry["x_tail"], cw_ref[...], cb_ref[...])
        carry["x_tail"] = x_rec[blk - SUBLANES:, :]

    def lru_gates(b):
        xc = ctx[b]["xc"]
        xcb = xc.astype(BF16)
        pre = [_dot(xcb[:, i * blk:(i + 1) * blk], wg_ref[i]) for i in range(n_gb)]
        r_pre = jnp.concatenate([p[:, :blk] for p in pre], axis=1)
        i_pre = jnp.concatenate([p[:, blk:] for p in pre], axis=1)
        gate_r = jax.nn.sigmoid(r_pre + ba_ref[...])
        gate_i = jax.nn.sigmoid(i_pre + bx_ref[...])
        log_a = log_a_scale * gate_r
        a = jnp.exp(log_a)
        y = -jnp.tanh(log_a) * (1.0 + a * a)
        mult = y * lax.rsqrt(jnp.maximum(y, TINY))
        ctx[b]["a"] = a
        ctx[b]["u"] = mult * (gate_i * xc)

    def lru_scan(b):
        ctx[b]["h_lru"], carry["h"] = _linear_scan(ctx[b]["a"], ctx[b]["u"], carry["h"])

    def lru_out(b):
        ctx[b]["out_a"] = (jax.nn.gelu(z_refs[b][:, 0:lw], approximate=True) * ctx[b]["h_lru"]).astype(BF16)

    def hg_decay(b):
        f = lb + (1.0 - lb) * jax.nn.sigmoid(z_refs[b][:, 2 * lw + hw:2 * lw + 2 * hw])
        logf = jnp.log(f)
        hi = logf.astype(BF16)
        lo = (logf - hi.astype(F32)).astype(BF16)
        ctx[b]["k"] = 1.0 - f
        ctx[b]["bcum"] = _dot(tri, hi) + _dot(tri, lo)
        ctx[b]["qf"] = _silu(z_refs[b][:, 2 * lw:2 * lw + hw])

    def hg_factors(b):
        bcum, qf, k = ctx[b]["bcum"], ctx[b]["qf"], ctx[b]["k"]
        qs, ks, qin, kd, dec = [], [], [], [], []
        for c in range(n_c):
            cr = slice(c * HG_CHUNK, (c + 1) * HG_CHUNK)
            bc = bcum[cr]
            bm = bc[HG_CHUNK // 2 - 1:HG_CHUNK // 2]
            bl = bc[HG_CHUNK - 1:HG_CHUNK]
            e_c = jnp.exp(bc - bm)
            qs_c = qf[cr] * e_c
            ks_c = k[cr] * (1.0 / e_c)
            qs.append(qs_c)
            ks.append(ks_c)
            qin.append((qs_c * jnp.exp(bm)).astype(BF16))
            kd.append((ks_c * jnp.exp(bl - bm)).astype(BF16))
            dec.append(jnp.exp(bl))
        ctx[b].update(qs=jnp.concatenate(qs, axis=0).astype(BF16), ks=jnp.concatenate(ks, axis=0).astype(BF16),
                      qin=qin, kd=kd, dec=dec,
                      vb=z_refs[b][:, 2 * lw + 2 * hw:2 * lw + 3 * hw].astype(BF16),
                      gg=_silu(z_refs[b][:, 2 * lw + 3 * hw:2 * lw + 4 * hw]), out_b=[])

    def hg_head(b, hd):
        c_ = ctx[b]
        cs = slice(hd * dk, (hd + 1) * dk)
        vb = c_["vb"]
        sc = _dot_nt(c_["qs"][:, cs], c_["ks"][:, cs])
        kv = [_dot_tn(vb[c * HG_CHUNK:(c + 1) * HG_CHUNK, cs], c_["kd"][c][:, cs]) for c in range(n_c)]
        st = carry["st"][hd]
        st_in = []
        for c in range(n_c):
            st_in.append(st.astype(BF16))
            st = st * c_["dec"][c][:, cs] + kv[c]
        carry["st"][hd] = st
        o = _dot(jnp.where(tri_mask, sc, 0.0).astype(BF16), vb[:, cs])
        o_inter = [_dot_nt(c_["qin"][c][:, cs], st_in[c]) for c in range(n_c)]
        o = _rms(o + jnp.concatenate(o_inter, axis=0), gn)
        c_["out_b"].append((o * c_["gg"][:, cs]).astype(BF16))

    def out_proj(b, j):
        if "mix" not in ctx[b]:
            ctx[b]["mix"] = jnp.concatenate([ctx[b]["out_a"]] + ctx[b]["out_b"], axis=1)
        rows = row_blocks[b]
        cols = slice(j * ow, (j + 1) * ow)
        o_ref[rows, cols] = x_ref[rows, cols] + _dot(ctx[b]["mix"], wout_ref[j])

    P = functools.partial
    proj_b = projection(lambda: x_ref[blk:2 * blk, :], zb_ref)
    proj_next = projection(lambda: xnext_ref[...], za_ref)
    heads0 = [P(hg_head, 0, hd) for hd in range(HG_HEADS)]
    heads1 = [P(hg_head, 1, hd) for hd in range(HG_HEADS)]
    outs0 = [P(out_proj, 0, j) for j in range(n_out)]
    outs1 = [P(out_proj, 1, j) for j in range(n_out)]
    schedule = [
        P(lru_conv, 0), proj_b[0], P(lru_gates, 0), proj_b[1], P(lru_scan, 0), proj_b[2], P(lru_out, 0), proj_b[3],
        P(hg_decay, 0), proj_b[4], P(hg_factors, 0), proj_b[5],
        P(lru_conv, 1), heads0[0], heads0[1], P(lru_gates, 1), heads0[2], heads0[3],
        P(lru_scan, 1), outs0[0], P(lru_out, 1), outs0[1],
        P(hg_decay, 1), proj_next[0], P(hg_factors, 1), proj_next[1],
        heads1[0], proj_next[2], heads1[1], proj_next[3], heads1[2], proj_next[4], heads1[3], proj_next[5],
        outs1[0], outs1[1],
    ]
    assert n_proj == 6 and n_out == 2
    for stage in schedule:
        stage()

    xtail_ref[...] = carry["x_tail"]
    hprev_ref[...] = jnp.broadcast_to(carry["h"], hprev_ref.shape)
    for hd in range(HG_HEADS):
        st_ref[hd] = carry["st"][hd]


def _block_diag(w):
    n, bi, bj = w.shape
    eye = jnp.eye(n, dtype=w.dtype)
    return (w[:, :, None, :] * eye[:, None, :, None]).reshape(n * bi, n * bj)


def _even_mixer(h, g, w_in, conv_w, conv_b, ga_w, ga_b, gx_w, gx_b, lam, lb_logits, hg_norm, w_out, *, layer):
    bsz, seq, d = h.shape
    lw = conv_w.shape[1]
    hw = lb_logits.shape[1]
    dk = hw // HG_HEADS
    blk = MXU_EDGE
    tt = 2 * blk
    assert lw % blk == 0 and seq % tt == 0 and hg_norm.shape[0] == dk
    assert w_in.shape[1] == 2 * lw + 4 * hw and w_out.shape[0] == lw + hw and lw == hw
    n_gb = lw // blk
    bda = _block_diag(ga_w)
    bdx = _block_diag(gx_w)
    wg = jnp.stack([
        jnp.concatenate([bda[i * blk:(i + 1) * blk, i * blk:(i + 1) * blk],
                         bdx[i * blk:(i + 1) * blk, i * blk:(i + 1) * blk]], axis=1)
        for i in range(n_gb)]).astype(BF16)
    win = _col_blocks(w_in, lw).astype(BF16)
    wout = _col_blocks(w_out, 2 * blk).astype(BF16)
    n_tiles = bsz * seq // tt
    x2 = h.reshape(bsz * seq, d)
    kern = functools.partial(_even_kernel, layer=layer, lw=lw, hw=hw, tiles_per_row=seq // tt)

    def cspec(shape):
        nd = len(shape)
        return pl.BlockSpec(shape, lambda s: (0,) * nd, pipeline_mode=pl.Buffered(1))

    out = pl.pallas_call(
        kern,
        grid=(n_tiles,),
        in_specs=[
            pl.BlockSpec((tt, d), lambda s: (s, 0)),
            pl.BlockSpec((blk, d), lambda s: (jnp.minimum(2 * s + 2, 2 * n_tiles - 1), 0)),
            cspec((1, d)),
            cspec(win.shape),
            cspec(conv_w.shape),
            cspec((1, lw)),
            cspec(wg.shape),
            cspec((1, lw)),
            cspec((1, lw)),
            cspec((1, lw)),
            cspec(lb_logits.shape),
            cspec((1, dk)),
            cspec(wout.shape),
        ],
        out_specs=pl.BlockSpec((tt, d), lambda s: (s, 0)),
        out_shape=jax.ShapeDtypeStruct(x2.shape, x2.dtype),
        scratch_shapes=[
            pltpu.VMEM((blk, w_in.shape[1]), F32),
            pltpu.VMEM((blk, w_in.shape[1]), F32),
            pltpu.VMEM((HG_HEADS, dk, dk), F32),
            pltpu.VMEM((SUBLANES, lw), F32),
            pltpu.VMEM((SUBLANES, lw), F32),
        ],
        compiler_params=pltpu.CompilerParams(
            dimension_semantics=("arbitrary",), vmem_limit_bytes=VMEM_LIMIT_BYTES),
        name="lru_hgrn_mixer",
    )(x2, x2, g.reshape(1, d), win, conv_w, conv_b.reshape(1, lw), wg, ga_b.reshape(1, lw),
      gx_b.reshape(1, lw), lam.reshape(1, lw), lb_logits, hg_norm.reshape(1, dk), wout)
    return out.reshape(bsz, seq, d)


def kernel(x, norm_mix, norm_ffn, norm_final, ev_w_in, ev_conv_w, ev_conv_b, ev_gate_a_w, ev_gate_a_b, ev_gate_x_w, ev_gate_x_b, ev_lru_lambda, hg_lb_logits, ev_hg_norm, ev_w_out, od_w_in, od_b_in, od_ln_g, od_ln_b, od_w_s, od_b_s, od_w_out, ffn_w_up, ffn_conv_w, ffn_conv_b, ffn_w_down):
    depth = norm_mix.shape[0]
    tt = min(TIME_TILE, x.shape[1])
    h = x
    for layer in range(depth):
        if layer % 2 == 0:
            e = layer // 2
            h = _even_mixer(h, norm_mix[layer], ev_w_in[e], ev_conv_w[e], ev_conv_b[e], ev_gate_a_w[e],
                            ev_gate_a_b[e], ev_gate_x_w[e], ev_gate_x_b[e], ev_lru_lambda[e], hg_lb_logits,
                            ev_hg_norm[e], ev_w_out[e], layer=layer)
        else:
            o = layer // 2
            h = _odd_mixer(h, norm_mix[layer], od_w_in[o], od_b_in[o], od_ln_g[o], od_ln_b[o], od_w_s[o],
                           od_b_s[o], od_w_out[o], tt=tt)
        h = _ffn(h, norm_ffn[layer], ffn_w_up[layer], ffn_conv_w[layer], ffn_conv_b[layer], ffn_w_down[layer],
                 norm_final, final=(layer == depth - 1), tt=tt)
    return h
```

```python
import functools

import jax
import jax.numpy as jnp
from jax import lax
from jax.experimental import pallas as pl
from jax.experimental.pallas import tpu as pltpu

F32 = jnp.float32
BF16 = jnp.bfloat16

EPS = 1e-6
LRU_C = 8.0
HG_HEADS = 4
HG_CHUNK = 64

MXU_EDGE = 256
SUBLANES = 8
VMEM_LIMIT_BYTES = 60 * 1024 * 1024
TINY = 1.1754944e-38

TIME_TILE = 1024
EVEN_BLOCKS_PER_TILE = TIME_TILE // MXU_EDGE


def _rms(x, g):
    return x * lax.rsqrt(jnp.mean(x * x, axis=-1, keepdims=True) + EPS) * g


def _dot(a, b):
    return jnp.dot(a, b, preferred_element_type=F32)


def _dot_nt(a, b):
    return lax.dot_general(a, b, (((1,), (1,)), ((), ())), preferred_element_type=F32)


def _dot_tn(a, b):
    return lax.dot_general(a, b, (((0,), (0,)), ((), ())), preferred_element_type=F32)


def _silu(x):
    return x * jax.nn.sigmoid(x)


def _causal_conv(x, tail, w, b):
    k_w = w.shape[0]
    out = b + w[k_w - 1:k_w] * x
    xx = jnp.concatenate([tail, x], axis=0)
    for s in range(1, k_w):
        out = out + w[k_w - 1 - s:k_w - s] * pltpu.roll(xx, s, 0)[SUBLANES:, :]
    return out


def _col_blocks(w, width):
    k, n = w.shape
    return jnp.transpose(w.reshape(k, n // width, width), (1, 0, 2))


def _ffn_kernel(h_ref, g_ref, wup_ref, cw_ref, cb_ref, wdn_ref, gfin_ref, o_ref, act_ref, tail_ref,
                *, n_chunks, cw, final):
    @pl.when(pl.program_id(1) == 0)
    def _():
        tail_ref[...] = jnp.zeros_like(tail_ref)

    h = h_ref[0]
    tt = h.shape[0]
    hn = _rms(h, g_ref[...]).astype(BF16)
    for j in range(n_chunks):
        cols = slice(j * cw, (j + 1) * cw)
        gate = _dot(hn, wup_ref[j])
        up = _dot(hn, wup_ref[n_chunks + j])
        c = _causal_conv(gate, tail_ref[:, cols], cw_ref[:, cols], cb_ref[:, cols])
        tail_ref[:, cols] = gate[tt - SUBLANES:, :]
        act_ref[:, cols] = (_silu(c) * up).astype(BF16)
    out = h + _dot(act_ref[...], wdn_ref[...])
    if final:
        out = _rms(out, gfin_ref[...])
    o_ref[0] = out


def _const_spec(shape):
    nd = len(shape)
    return pl.BlockSpec(shape, lambda b, t: (0,) * nd, pipeline_mode=pl.Buffered(1))


def _ffn_up_blocks(w_up_all):
    depth, d, n = w_up_all.shape
    return jnp.transpose(w_up_all.reshape(depth, d, n // MXU_EDGE, MXU_EDGE), (0, 2, 1, 3)).astype(BF16)


def _ffn(h, g, wup_all, conv_w, conv_b, wdn_all, g_final, *, layer, final, tt):
    bsz, seq, d = h.shape
    d_ff = wdn_all.shape[1]
    cw = MXU_EDGE
    n_chunks = d_ff // cw
    assert n_chunks * cw == d_ff and seq % tt == 0 and wup_all.shape[1:] == (2 * n_chunks, d, cw)
    kern = functools.partial(_ffn_kernel, n_chunks=n_chunks, cw=cw, final=final)
    return pl.pallas_call(
        kern,
        grid=(bsz, seq // tt),
        in_specs=[
            pl.BlockSpec((1, tt, d), lambda b, t: (b, t, 0)),
            _const_spec((1, d)),
            pl.BlockSpec((None, 2 * n_chunks, d, cw), lambda b, t: (layer, 0, 0, 0), pipeline_mode=pl.Buffered(1)),
            _const_spec(conv_w.shape),
            _const_spec((1, d_ff)),
            pl.BlockSpec((None, d_ff, d), lambda b, t: (layer, 0, 0), pipeline_mode=pl.Buffered(1)),
            _const_spec((1, d)),
        ],
        out_specs=pl.BlockSpec((1, tt, d), lambda b, t: (b, t, 0)),
        out_shape=jax.ShapeDtypeStruct(h.shape, h.dtype),
        scratch_shapes=[
            pltpu.VMEM((tt, d_ff), BF16),
            pltpu.VMEM((SUBLANES, d_ff), F32),
        ],
        compiler_params=pltpu.CompilerParams(
            dimension_semantics=("parallel", "arbitrary"), vmem_limit_bytes=VMEM_LIMIT_BYTES),
        name="conv_ffn",
    )(h, g.reshape(1, d), wup_all, conv_w, conv_b.reshape(1, d_ff), wdn_all, g_final.reshape(1, d))


def _odd_kernel(h_ref, g_ref, win_ref, bin_ref, lng_ref, lnb_ref, ws_ref, bias_ref, wout_ref, o_ref,
                bd_ref, *, width, groups, chunk):
    gw = width // groups
    blk = 2 * chunk
    n_proj, _, pw = win_ref.shape
    half = n_proj // 2

    @pl.when(pl.program_id(1) == 0)
    def _():
        r = lax.broadcasted_iota(jnp.int32, (chunk, chunk), 0)
        c = lax.broadcasted_iota(jnp.int32, (chunk, chunk), 1)
        bd_ref[...] = jnp.zeros_like(bd_ref)
        for g in range(groups):
            wc = jnp.where(r >= c, ws_ref[g], 0.0).astype(BF16)
            bd_ref[g, 0:chunk, 0:chunk] = wc
            bd_ref[g, chunk:blk, chunk:blk] = wc

    n_blk = h_ref.shape[1] // blk
    ctx = [dict() for _ in range(n_blk)]

    def proj(b):
        rows = slice(b * blk, (b + 1) * blk)
        hn = _rms(h_ref[0, rows, :], g_ref[...]).astype(BF16)
        ctx[b]["z"] = [_dot(hn, win_ref[j]) for j in range(n_proj)]

    def act(b):
        zs = [jax.nn.gelu(ctx[b]["z"][j] + bin_ref[:, j * pw:(j + 1) * pw], approximate=True)
              for j in range(n_proj)]
        ctx[b]["u"] = jnp.concatenate(zs[:half], axis=1)
        v = jnp.concatenate(zs[half:], axis=1)
        vc = v - jnp.mean(v, axis=-1, keepdims=True)
        var = jnp.mean(vc * vc, axis=-1, keepdims=True)
        ctx[b]["vn"] = (vc * lax.rsqrt(var + EPS) * lng_ref[...] + lnb_ref[...]).astype(BF16)

    def spatial_out(b):
        rows = slice(b * blk, (b + 1) * blk)
        gated = []
        for g in range(groups):
            cols = slice(g * gw, (g + 1) * gw)
            sv = _dot(bd_ref[g], ctx[b]["vn"][:, cols]) + bias_ref[:, cols]
            gated.append((ctx[b]["u"][:, cols] * sv).astype(BF16))
        o_ref[0, rows, :] = h_ref[0, rows, :] + _dot(jnp.concatenate(gated, axis=1), wout_ref[...])

    proj(0)
    for b in range(n_blk):
        act(b)
        if b + 1 < n_blk:
            proj(b + 1)
        if b >= 1:
            spatial_out(b - 1)
    spatial_out(n_blk - 1)


def _odd_mixer(h, g, w_in, b_in, ln_g, ln_b, w_s, b_s, w_out, *, tt):
    bsz, seq, d = h.shape
    width = w_out.shape[0]
    groups, chunk = w_s.shape[0], w_s.shape[1]
    gw = width // groups
    blk = 2 * chunk
    assert blk == MXU_EDGE and tt % blk == 0 and seq % tt == 0
    bias = jnp.tile(jnp.repeat(b_s.T, gw, axis=1), (2, 1))
    win = _col_blocks(w_in, 2 * blk).astype(BF16)
    kern = functools.partial(_odd_kernel, width=width, groups=groups, chunk=chunk)
    return pl.pallas_call(
        kern,
        grid=(bsz, seq // tt),
        in_specs=[
            pl.BlockSpec((1, tt, d), lambda b, t: (b, t, 0)),
            _const_spec((1, d)),
            _const_spec(win.shape),
            _const_spec((1, 2 * width)),
            _const_spec((1, width)),
            _const_spec((1, width)),
            _const_spec(w_s.shape),
            _const_spec((blk, width)),
            _const_spec((width, d)),
        ],
        out_specs=pl.BlockSpec((1, tt, d), lambda b, t: (b, t, 0)),
        out_shape=jax.ShapeDtypeStruct(h.shape, h.dtype),
        scratch_shapes=[
            pltpu.VMEM((groups, blk, blk), BF16),
        ],
        compiler_params=pltpu.CompilerParams(
            dimension_semantics=("parallel", "arbitrary"), vmem_limit_bytes=VMEM_LIMIT_BYTES),
        name="sgu_mixer",
    )(h, g.reshape(1, d), win, b_in.reshape(1, -1), ln_g.reshape(1, -1), ln_b.reshape(1, -1),
      w_s, bias, w_out.astype(BF16))


def _linear_scan(a, u, carry):
    n, c = a.shape
    groups = n // SUBLANES
    a3 = a.reshape(groups, SUBLANES, c)
    u3 = u.reshape(groups, SUBLANES, c)
    sub = lax.broadcasted_iota(jnp.int32, a3.shape, 1)
    s = 1
    while s < SUBLANES:
        m = sub >= s
        u3 = a3 * jnp.where(m, pltpu.roll(u3, s, 1), 0.0) + u3
        a3 = a3 * jnp.where(m, pltpu.roll(a3, s, 1), 1.0)
        s *= 2
    out = []
    for i in range(groups):
        hi = u3[i] + a3[i] * carry
        carry = hi[SUBLANES - 1:SUBLANES]
        out.append(hi)
    return jnp.concatenate(out, axis=0), carry


def _even_kernel(x_ref, xnext_ref, g_ref, win_ref, cw_ref, cb_ref, wg_ref, ba_ref, bx_ref, lam_ref, lbl_ref, gn_ref,
                 wout_ref, o_ref, *scratch, layer, lw, hw, tiles_per_row, n_blocks):
    dk = hw // HG_HEADS
    blk = MXU_EDGE
    n_c = blk // HG_CHUNK
    n_gb = lw // blk
    n_proj, _, pw = win_ref.shape
    n_out = wout_ref.shape[0]
    ow = wout_ref.shape[2]
    z_refs = tuple(scratch[:n_blocks])
    st_ref, hprev_ref, xtail_ref = scratch[n_blocks:]
    assert x_ref.shape[0] == n_blocks * blk
    row_blocks = tuple(slice(i * blk, (i + 1) * blk) for i in range(n_blocks))

    s = pl.program_id(0)

    def projection(src, dst_ref):
        cache = []

        def piece(j):
            if not cache:
                cache.append(_rms(src(), g_ref[...]).astype(BF16))
            dst_ref[:, j * pw:(j + 1) * pw] = _dot(cache[0], win_ref[j])

        return [functools.partial(piece, j) for j in range(n_proj)]

    @pl.when(s == 0)
    def _():
        for p in projection(lambda: x_ref[0:blk, :], z_refs[0]):
            p()

    @pl.when(s % tiles_per_row == 0)
    def _():
        st_ref[...] = jnp.zeros_like(st_ref)
        hprev_ref[...] = jnp.zeros_like(hprev_ref)
        xtail_ref[...] = jnp.zeros_like(xtail_ref)

    lam = lam_ref[...]
    softplus_neg_lam = jnp.maximum(-lam, 0.0) + jnp.log1p(jnp.exp(-jnp.abs(lam)))
    log_a_scale = -LRU_C * softplus_neg_lam

    lbl = lbl_ref[...]
    e = jnp.exp(lbl - jnp.max(lbl, axis=0, keepdims=True))
    sm = e / jnp.sum(e, axis=0, keepdims=True)
    lb = jnp.sum(sm[0:layer + 1], axis=0, keepdims=True)

    r_i = lax.broadcasted_iota(jnp.int32, (blk, blk), 0)
    c_i = lax.broadcasted_iota(jnp.int32, (blk, blk), 1)
    tri_mask = (r_i >= c_i) & ((r_i // HG_CHUNK) == (c_i // HG_CHUNK))
    tri = jnp.where(tri_mask, 1.0, 0.0).astype(BF16)
    gn = gn_ref[...]

    carry = {"x_tail": xtail_ref[...], "h": hprev_ref[0:1, :],
             "st": [st_ref[hd] for hd in range(HG_HEADS)]}
    ctx = tuple({} for _ in range(n_blocks))

    def lru_conv(b):
        x_rec = z_refs[b][:, lw:2 * lw]
        ctx[b]["xc"] = _causal_conv(x_rec, carry["x_tail"], cw_ref[...], cb_ref[...])
        carry["x_tail"] = x_rec[blk - SUBLANES:, :]

    def lru_gates(b):
        xc = ctx[b]["xc"]
        xcb = xc.astype(BF16)
        pre = [_dot(xcb[:, i * blk:(i + 1) * blk], wg_ref[i]) for i in range(n_gb)]
        r_pre = jnp.concatenate([p[:, :blk] for p in pre], axis=1)
        i_pre = jnp.concatenate([p[:, blk:] for p in pre], axis=1)
        gate_r = jax.nn.sigmoid(r_pre + ba_ref[...])
        gate_i = jax.nn.sigmoid(i_pre + bx_ref[...])
        log_a = log_a_scale * gate_r
        a = jnp.exp(log_a)
        y = -jnp.tanh(log_a) * (1.0 + a * a)
        mult = y * lax.rsqrt(jnp.maximum(y, TINY))
        ctx[b]["a"] = a
        ctx[b]["u"] = mult * (gate_i * xc)

    def lru_scan(b):
        ctx[b]["h_lru"], carry["h"] = _linear_scan(ctx[b]["a"], ctx[b]["u"], carry["h"])

    def lru_out(b):
        ctx[b]["out_a"] = (jax.nn.gelu(z_refs[b][:, 0:lw], approximate=True) * ctx[b]["h_lru"]).astype(BF16)

    def hg_decay(b):
        f = lb + (1.0 - lb) * jax.nn.sigmoid(z_refs[b][:, 2 * lw + hw:2 * lw + 2 * hw])
        logf = jnp.log(f)
        hi = logf.astype(BF16)
        lo = (logf - hi.astype(F32)).astype(BF16)
        ctx[b]["k"] = 1.0 - f
        ctx[b]["bcum"] = _dot(tri, hi) + _dot(tri, lo)
        ctx[b]["qf"] = _silu(z_refs[b][:, 2 * lw:2 * lw + hw])

    def hg_factors(b):
        bcum, qf, k = ctx[b]["bcum"], ctx[b]["qf"], ctx[b]["k"]
        qs, ks, qin, kd, dec = [], [], [], [], []
        for c in range(n_c):
            cr = slice(c * HG_CHUNK, (c + 1) * HG_CHUNK)
            bc = bcum[cr]
            bm = bc[HG_CHUNK // 2 - 1:HG_CHUNK // 2]
            bl = bc[HG_CHUNK - 1:HG_CHUNK]
            e_c = jnp.exp(bc - bm)
            qs_c = qf[cr] * e_c
            ks_c = k[cr] * (1.0 / e_c)
            qs.append(qs_c)
            ks.append(ks_c)
            qin.append((qs_c * jnp.exp(bm)).astype(BF16))
            kd.append((ks_c * jnp.exp(bl - bm)).astype(BF16))
            dec.append(jnp.exp(bl))
        ctx[b].update(qs=jnp.concatenate(qs, axis=0).astype(BF16), ks=jnp.concatenate(ks, axis=0).astype(BF16),
                      qin=qin, kd=kd, dec=dec,
                      vb=z_refs[b][:, 2 * lw + 2 * hw:2 * lw + 3 * hw].astype(BF16),
                      gg=_silu(z_refs[b][:, 2 * lw + 3 * hw:2 * lw + 4 * hw]), out_b=[])

    def hg_head(b, hd):
        c_ = ctx[b]
        cs = slice(hd * dk, (hd + 1) * dk)
        vb = c_["vb"]
        sc = _dot_nt(c_["qs"][:, cs], c_["ks"][:, cs])
        kv = [_dot_tn(vb[c * HG_CHUNK:(c + 1) * HG_CHUNK, cs], c_["kd"][c][:, cs]) for c in range(n_c)]
        st = carry["st"][hd]
        st_in = []
        for c in range(n_c):
            st_in.append(st.astype(BF16))
            st = st * c_["dec"][c][:, cs] + kv[c]
        carry["st"][hd] = st
        o = _dot(jnp.where(tri_mask, sc, 0.0).astype(BF16), vb[:, cs])
        o_inter = [_dot_nt(c_["qin"][c][:, cs], st_in[c]) for c in range(n_c)]
        o = _rms(o + jnp.concatenate(o_inter, axis=0), gn)
        c_["out_b"].append((o * c_["gg"][:, cs]).astype(BF16))

    def out_proj(b, j):
        if "mix" not in ctx[b]:
            ctx[b]["mix"] = jnp.concatenate([ctx[b]["out_a"]] + ctx[b]["out_b"], axis=1)
        rows = row_blocks[b]
        cols = slice(j * ow, (j + 1) * ow)
        o_ref[rows, cols] = x_ref[rows, cols] + _dot(ctx[b]["mix"], wout_ref[j])

    P = functools.partial
    projs = [projection(P(lambda i: x_ref[row_blocks[i], :], i), z_refs[i]) for i in range(1, n_blocks)]
    projs.append(projection(lambda: xnext_ref[...], z_refs[0]))
    assert n_proj == 6 and n_out == 2
    schedule = []
    for b in range(n_blocks):
        pp = projs[b]
        heads = [P(hg_head, b, hd) for hd in range(HG_HEADS)]
        outs = [P(out_proj, b, j) for j in range(n_out)]
        if b == 0:
            schedule += [P(lru_conv, 0), pp[0], P(lru_gates, 0), pp[1], P(lru_scan, 0), pp[2], P(lru_out, 0), pp[3],
                         P(hg_decay, 0), pp[4], P(hg_factors, 0), pp[5]]
        else:
            schedule += [P(lru_conv, b), prev_heads[0], prev_heads[1], P(lru_gates, b), prev_heads[2], prev_heads[3],
                         P(lru_scan, b), prev_outs[0], P(lru_out, b), prev_outs[1],
                         P(hg_decay, b), pp[0], P(hg_factors, b), pp[1]]
            if b + 1 < n_blocks:
                schedule += pp[2:]
            else:
                schedule += [heads[0], pp[2], heads[1], pp[3], heads[2], pp[4], heads[3], pp[5], outs[0], outs[1]]
        prev_heads, prev_outs = heads, outs
    for stage in schedule:
        stage()

    xtail_ref[...] = carry["x_tail"]
    hprev_ref[...] = jnp.broadcast_to(carry["h"], hprev_ref.shape)
    for hd in range(HG_HEADS):
        st_ref[hd] = carry["st"][hd]


def _block_diag(w):
    n, bi, bj = w.shape
    eye = jnp.eye(n, dtype=w.dtype)
    return (w[:, :, None, :] * eye[:, None, :, None]).reshape(n * bi, n * bj)


def _even_mixer(h, g, w_in, conv_w, conv_b, ga_w, ga_b, gx_w, gx_b, lam, lb_logits, hg_norm, w_out, *, layer):
    bsz, seq, d = h.shape
    lw = conv_w.shape[1]
    hw = lb_logits.shape[1]
    dk = hw // HG_HEADS
    blk = MXU_EDGE
    n_blocks = EVEN_BLOCKS_PER_TILE
    tt = n_blocks * blk
    assert lw % blk == 0 and seq % tt == 0 and hg_norm.shape[0] == dk
    assert w_in.shape[1] == 2 * lw + 4 * hw and w_out.shape[0] == lw + hw and lw == hw
    n_gb = lw // blk
    bda = _block_diag(ga_w)
    bdx = _block_diag(gx_w)
    wg = jnp.stack([
        jnp.concatenate([bda[i * blk:(i + 1) * blk, i * blk:(i + 1) * blk],
                         bdx[i * blk:(i + 1) * blk, i * blk:(i + 1) * blk]], axis=1)
        for i in range(n_gb)]).astype(BF16)
    win = _col_blocks(w_in, lw).astype(BF16)
    wout = _col_blocks(w_out, 2 * blk).astype(BF16)
    n_tiles = bsz * seq // tt
    x2 = h.reshape(bsz * seq, d)
    kern = functools.partial(_even_kernel, layer=layer, lw=lw, hw=hw, tiles_per_row=seq // tt, n_blocks=n_blocks)

    def cspec(shape):
        nd = len(shape)
        return pl.BlockSpec(shape, lambda s: (0,) * nd, pipeline_mode=pl.Buffered(1))

    out = pl.pallas_call(
        kern,
        grid=(n_tiles,),
        in_specs=[
            pl.BlockSpec((tt, d), lambda s: (s, 0)),
            pl.BlockSpec((blk, d), lambda s: (jnp.minimum(n_blocks * (s + 1), n_blocks * n_tiles - 1), 0)),
            cspec((1, d)),
            cspec(win.shape),
            cspec(conv_w.shape),
            cspec((1, lw)),
            cspec(wg.shape),
            cspec((1, lw)),
            cspec((1, lw)),
            cspec((1, lw)),
            cspec(lb_logits.shape),
            cspec((1, dk)),
            cspec(wout.shape),
        ],
        out_specs=pl.BlockSpec((tt, d), lambda s: (s, 0)),
        out_shape=jax.ShapeDtypeStruct(x2.shape, x2.dtype),
        scratch_shapes=[pltpu.VMEM((blk, w_in.shape[1]), F32)] * n_blocks + [
            pltpu.VMEM((HG_HEADS, dk, dk), F32),
            pltpu.VMEM((SUBLANES, lw), F32),
            pltpu.VMEM((SUBLANES, lw), F32),
        ],
        compiler_params=pltpu.CompilerParams(
            dimension_semantics=("arbitrary",), vmem_limit_bytes=VMEM_LIMIT_BYTES),
        name="lru_hgrn_mixer",
    )(x2, x2, g.reshape(1, d), win, conv_w, conv_b.reshape(1, lw), wg, ga_b.reshape(1, lw),
      gx_b.reshape(1, lw), lam.reshape(1, lw), lb_logits, hg_norm.reshape(1, dk), wout)
    return out.reshape(bsz, seq, d)


def kernel(x, norm_mix, norm_ffn, norm_final, ev_w_in, ev_conv_w, ev_conv_b, ev_gate_a_w, ev_gate_a_b, ev_gate_x_w, ev_gate_x_b, ev_lru_lambda, hg_lb_logits, ev_hg_norm, ev_w_out, od_w_in, od_b_in, od_ln_g, od_ln_b, od_w_s, od_b_s, od_w_out, ffn_w_up, ffn_conv_w, ffn_conv_b, ffn_w_down):
    depth = norm_mix.shape[0]
    tt = min(TIME_TILE, x.shape[1])
    wup_all = _ffn_up_blocks(ffn_w_up)
    wdn_all = ffn_w_down.astype(BF16)
    h = x
    for layer in range(depth):
        if layer % 2 == 0:
            e = layer // 2
            h = _even_mixer(h, norm_mix[layer], ev_w_in[e], ev_conv_w[e], ev_conv_b[e], ev_gate_a_w[e],
                            ev_gate_a_b[e], ev_gate_x_w[e], ev_gate_x_b[e], ev_lru_lambda[e], hg_lb_logits,
                            ev_hg_norm[e], ev_w_out[e], layer=layer)
        else:
            o = layer // 2
            h = _odd_mixer(h, norm_mix[layer], od_w_in[o], od_b_in[o], od_ln_g[o], od_ln_b[o], od_w_s[o],
                           od_b_s[o], od_w_out[o], tt=tt)
        h = _ffn(h, norm_ffn[layer], wup_all, ffn_conv_w[layer], ffn_conv_b[layer], wdn_all,
                 norm_final, layer=layer, final=(layer == depth - 1), tt=tt)
    return h
```

```python
import functools

import jax
import jax.numpy as jnp
from jax import lax
from jax.experimental import pallas as pl
from jax.experimental.pallas import tpu as pltpu

F32 = jnp.float32
BF16 = jnp.bfloat16

EPS = 1e-6
LRU_C = 8.0
HG_HEADS = 4
HG_CHUNK = 64

MXU_EDGE = 256
SUBLANES = 8
VMEM_LIMIT_BYTES = 60 * 1024 * 1024
TINY = 1.1754944e-38

TIME_TILE = 1024
EVEN_BLOCKS_PER_TILE = TIME_TILE // MXU_EDGE


def _rms(x, g):
    return x * lax.rsqrt(jnp.mean(x * x, axis=-1, keepdims=True) + EPS) * g


def _dot(a, b):
    return jnp.dot(a, b, preferred_element_type=F32)


def _dot_nt(a, b):
    return lax.dot_general(a, b, (((1,), (1,)), ((), ())), preferred_element_type=F32)


def _dot_tn(a, b):
    return lax.dot_general(a, b, (((0,), (0,)), ((), ())), preferred_element_type=F32)


def _silu(x):
    return x * jax.nn.sigmoid(x)


def _causal_conv(x, tail, w, b):
    k_w = w.shape[0]
    out = b + w[k_w - 1:k_w] * x
    xx = jnp.concatenate([tail, x], axis=0)
    for s in range(1, k_w):
        out = out + w[k_w - 1 - s:k_w - s] * pltpu.roll(xx, s, 0)[SUBLANES:, :]
    return out


def _col_blocks(w, width):
    k, n = w.shape
    return jnp.transpose(w.reshape(k, n // width, width), (1, 0, 2))


def _ffn_kernel(h_ref, g_ref, wup_ref, cw_ref, cb_ref, wdn_ref, gfin_ref, o_ref, act_ref, tail_ref,
                *, n_chunks, cw, final, row_blocks):
    @pl.when(pl.program_id(1) == 0)
    def _():
        tail_ref[...] = jnp.zeros_like(tail_ref)

    rb = h_ref.shape[1] // row_blocks

    def up(r):
        rows = slice(r * rb, (r + 1) * rb)
        hn = _rms(h_ref[0, rows, :], g_ref[...]).astype(BF16)
        for j in range(n_chunks):
            cols = slice(j * cw, (j + 1) * cw)
            gate = _dot(hn, wup_ref[j])
            up_j = _dot(hn, wup_ref[n_chunks + j])
            c = _causal_conv(gate, tail_ref[:, cols], cw_ref[:, cols], cb_ref[:, cols])
            tail_ref[:, cols] = gate[rb - SUBLANES:, :]
            act_ref[rows, cols] = (_silu(c) * up_j).astype(BF16)

    def down(r):
        rows = slice(r * rb, (r + 1) * rb)
        out = h_ref[0, rows, :] + _dot(act_ref[rows, :], wdn_ref[...])
        if final:
            out = _rms(out, gfin_ref[...])
        o_ref[0, rows, :] = out

    for r in range(row_blocks):
        up(r)
        if r >= 1:
            down(r - 1)
    down(row_blocks - 1)


def _const_spec(shape):
    nd = len(shape)
    return pl.BlockSpec(shape, lambda b, t: (0,) * nd, pipeline_mode=pl.Buffered(1))


def _ffn_up_blocks(w_up_all):
    depth, d, n = w_up_all.shape
    return jnp.transpose(w_up_all.reshape(depth, d, n // MXU_EDGE, MXU_EDGE), (0, 2, 1, 3)).astype(BF16)


def _ffn(h, g, wup_all, conv_w, conv_b, wdn_all, g_final, *, layer, final, tt):
    bsz, seq, d = h.shape
    d_ff = wdn_all.shape[1]
    cw = MXU_EDGE
    n_chunks = d_ff // cw
    assert n_chunks * cw == d_ff and seq % tt == 0 and wup_all.shape[1:] == (2 * n_chunks, d, cw)
    assert tt % MXU_EDGE == 0
    kern = functools.partial(_ffn_kernel, n_chunks=n_chunks, cw=cw, final=final, row_blocks=tt // MXU_EDGE)
    return pl.pallas_call(
        kern,
        grid=(bsz, seq // tt),
        in_specs=[
            pl.BlockSpec((1, tt, d), lambda b, t: (b, t, 0)),
            _const_spec((1, d)),
            pl.BlockSpec((None, 2 * n_chunks, d, cw), lambda b, t: (layer, 0, 0, 0), pipeline_mode=pl.Buffered(1)),
            _const_spec(conv_w.shape),
            _const_spec((1, d_ff)),
            pl.BlockSpec((None, d_ff, d), lambda b, t: (layer, 0, 0), pipeline_mode=pl.Buffered(1)),
            _const_spec((1, d)),
        ],
        out_specs=pl.BlockSpec((1, tt, d), lambda b, t: (b, t, 0)),
        out_shape=jax.ShapeDtypeStruct(h.shape, h.dtype),
        scratch_shapes=[
            pltpu.VMEM((tt, d_ff), BF16),
            pltpu.VMEM((SUBLANES, d_ff), F32),
        ],
        compiler_params=pltpu.CompilerParams(
            dimension_semantics=("parallel", "arbitrary"), vmem_limit_bytes=VMEM_LIMIT_BYTES),
        name="conv_ffn",
    )(h, g.reshape(1, d), wup_all, conv_w, conv_b.reshape(1, d_ff), wdn_all, g_final.reshape(1, d))


def _odd_kernel(h_ref, g_ref, win_ref, bin_ref, lng_ref, lnb_ref, ws_ref, bias_ref, wout_ref, o_ref,
                bd_ref, *, width, groups, chunk):
    gw = width // groups
    blk = 2 * chunk
    n_proj, _, pw = win_ref.shape
    half = n_proj // 2

    @pl.when(pl.program_id(1) == 0)
    def _():
        r = lax.broadcasted_iota(jnp.int32, (chunk, chunk), 0)
        c = lax.broadcasted_iota(jnp.int32, (chunk, chunk), 1)
        bd_ref[...] = jnp.zeros_like(bd_ref)
        for g in range(groups):
            wc = jnp.where(r >= c, ws_ref[g], 0.0).astype(BF16)
            bd_ref[g, 0:chunk, 0:chunk] = wc
            bd_ref[g, chunk:blk, chunk:blk] = wc

    n_blk = h_ref.shape[1] // blk
    ctx = [dict() for _ in range(n_blk)]

    def proj(b):
        rows = slice(b * blk, (b + 1) * blk)
        hn = _rms(h_ref[0, rows, :], g_ref[...]).astype(BF16)
        ctx[b]["z"] = [_dot(hn, win_ref[j]) for j in range(n_proj)]

    def act(b):
        zs = [jax.nn.gelu(ctx[b]["z"][j] + bin_ref[:, j * pw:(j + 1) * pw], approximate=True)
              for j in range(n_proj)]
        ctx[b]["u"] = jnp.concatenate(zs[:half], axis=1)
        v = jnp.concatenate(zs[half:], axis=1)
        vc = v - jnp.mean(v, axis=-1, keepdims=True)
        var = jnp.mean(vc * vc, axis=-1, keepdims=True)
        ctx[b]["vn"] = (vc * lax.rsqrt(var + EPS) * lng_ref[...] + lnb_ref[...]).astype(BF16)

    def spatial(b0):
        pair = (b0, b0 + 1)
        gated = ([], [])
        for g in range(groups):
            cols = slice(g * gw, (g + 1) * gw)
            sv = _dot(bd_ref[g], jnp.concatenate([ctx[b]["vn"][:, cols] for b in pair], axis=1))
            for i, b in enumerate(pair):
                sv_b = sv[:, i * gw:(i + 1) * gw] + bias_ref[:, cols]
                gated[i].append((ctx[b]["u"][:, cols] * sv_b).astype(BF16))
        for i, b in enumerate(pair):
            ctx[b]["gated"] = jnp.concatenate(gated[i], axis=1)

    def out(b):
        rows = slice(b * blk, (b + 1) * blk)
        o_ref[0, rows, :] = h_ref[0, rows, :] + _dot(ctx[b]["gated"], wout_ref[...])

    assert n_blk % 2 == 0
    proj(0)
    for b in range(n_blk):
        act(b)
        if b + 1 < n_blk:
            proj(b + 1)
        if b % 2 == 1:
            spatial(b - 1)
            out(b - 1)
            out(b)


def _odd_mixer(h, g, w_in, b_in, ln_g, ln_b, w_s, b_s, w_out, *, tt):
    bsz, seq, d = h.shape
    width = w_out.shape[0]
    groups, chunk = w_s.shape[0], w_s.shape[1]
    gw = width // groups
    blk = 2 * chunk
    assert blk == MXU_EDGE and tt % blk == 0 and seq % tt == 0
    bias = jnp.tile(jnp.repeat(b_s.T, gw, axis=1), (2, 1))
    win = _col_blocks(w_in, 2 * blk).astype(BF16)
    kern = functools.partial(_odd_kernel, width=width, groups=groups, chunk=chunk)
    return pl.pallas_call(
        kern,
        grid=(bsz, seq // tt),
        in_specs=[
            pl.BlockSpec((1, tt, d), lambda b, t: (b, t, 0)),
            _const_spec((1, d)),
            _const_spec(win.shape),
            _const_spec((1, 2 * width)),
            _const_spec((1, width)),
            _const_spec((1, width)),
            _const_spec(w_s.shape),
            _const_spec((blk, width)),
            _const_spec((width, d)),
        ],
        out_specs=pl.BlockSpec((1, tt, d), lambda b, t: (b, t, 0)),
        out_shape=jax.ShapeDtypeStruct(h.shape, h.dtype),
        scratch_shapes=[
            pltpu.VMEM((groups, blk, blk), BF16),
        ],
        compiler_params=pltpu.CompilerParams(
            dimension_semantics=("parallel", "arbitrary"), vmem_limit_bytes=VMEM_LIMIT_BYTES),
        name="sgu_mixer",
    )(h, g.reshape(1, d), win, b_in.reshape(1, -1), ln_g.reshape(1, -1), ln_b.reshape(1, -1),
      w_s, bias, w_out.astype(BF16))


def _linear_scan(a, u, carry):
    n, c = a.shape
    groups = n // SUBLANES
    a3 = a.reshape(groups, SUBLANES, c)
    u3 = u.reshape(groups, SUBLANES, c)
    sub = lax.broadcasted_iota(jnp.int32, a3.shape, 1)
    s = 1
    while s < SUBLANES:
        m = sub >= s
        u3 = a3 * jnp.where(m, pltpu.roll(u3, s, 1), 0.0) + u3
        a3 = a3 * jnp.where(m, pltpu.roll(a3, s, 1), 1.0)
        s *= 2
    out = []
    for i in range(groups):
        hi = u3[i] + a3[i] * carry
        carry = hi[SUBLANES - 1:SUBLANES]
        out.append(hi)
    return jnp.concatenate(out, axis=0), carry


def _even_kernel(x_ref, xnext_ref, g_ref, win_ref, cw_ref, cb_ref, wg_ref, ba_ref, bx_ref, lam_ref, lbl_ref, gn_ref,
                 wout_ref, o_ref, *scratch, layer, lw, hw, tiles_per_row, n_blocks):
    dk = hw // HG_HEADS
    blk = MXU_EDGE
    n_c = blk // HG_CHUNK
    n_gb = lw // blk
    n_proj, _, pw = win_ref.shape
    n_out = wout_ref.shape[0]
    ow = wout_ref.shape[2]
    z_refs = tuple(scratch[:n_blocks])
    st_ref, hprev_ref, xtail_ref = scratch[n_blocks:]
    assert x_ref.shape[0] == n_blocks * blk
    row_blocks = tuple(slice(i * blk, (i + 1) * blk) for i in range(n_blocks))

    s = pl.program_id(0)

    def projection(src, dst_ref):
        cache = []

        def piece(j):
            if not cache:
                cache.append(_rms(src(), g_ref[...]).astype(BF16))
            dst_ref[:, j * pw:(j + 1) * pw] = _dot(cache[0], win_ref[j])

        return [functools.partial(piece, j) for j in range(n_proj)]

    @pl.when(s == 0)
    def _():
        for p in projection(lambda: x_ref[0:blk, :], z_refs[0]):
            p()

    @pl.when(s % tiles_per_row == 0)
    def _():
        st_ref[...] = jnp.zeros_like(st_ref)
        hprev_ref[...] = jnp.zeros_like(hprev_ref)
        xtail_ref[...] = jnp.zeros_like(xtail_ref)

    lam = lam_ref[...]
    softplus_neg_lam = jnp.maximum(-lam, 0.0) + jnp.log1p(jnp.exp(-jnp.abs(lam)))
    log_a_scale = -LRU_C * softplus_neg_lam

    lbl = lbl_ref[...]
    e = jnp.exp(lbl - jnp.max(lbl, axis=0, keepdims=True))
    sm = e / jnp.sum(e, axis=0, keepdims=True)
    lb = jnp.sum(sm[0:layer + 1], axis=0, keepdims=True)

    r_i = lax.broadcasted_iota(jnp.int32, (blk, blk), 0)
    c_i = lax.broadcasted_iota(jnp.int32, (blk, blk), 1)
    tri_mask = (r_i >= c_i) & ((r_i // HG_CHUNK) == (c_i // HG_CHUNK))
    tri = jnp.where(tri_mask, 1.0, 0.0).astype(BF16)
    gn = gn_ref[...]

    carry = {"x_tail": xtail_ref[...], "h": hprev_ref[0:1, :],
             "st": [st_ref[hd] for hd in range(HG_HEADS)]}
    ctx = tuple({} for _ in range(n_blocks))

    def lru_conv(b):
        x_rec = z_refs[b][:, lw:2 * lw]
        ctx[b]["xc"] = _causal_conv(x_rec, carry["x_tail"], cw_ref[...], cb_ref[...])
        carry["x_tail"] = x_rec[blk - SUBLANES:, :]

    def lru_gates(b):
        xc = ctx[b]["xc"]
        xcb = xc.astype(BF16)
        pre = [_dot(xcb[:, i * blk:(i + 1) * blk], wg_ref[i]) for i in range(n_gb)]
        r_pre = jnp.concatenate([p[:, :blk] for p in pre], axis=1)
        i_pre = jnp.concatenate([p[:, blk:] for p in pre], axis=1)
        gate_r = jax.nn.sigmoid(r_pre + ba_ref[...])
        gate_i = jax.nn.sigmoid(i_pre + bx_ref[...])
        log_a = log_a_scale * gate_r
        a = jnp.exp(log_a)
        y = -jnp.tanh(log_a) * (1.0 + a * a)
        mult = y * lax.rsqrt(jnp.maximum(y, TINY))
        ctx[b]["a"] = a
        ctx[b]["u"] = mult * (gate_i * xc)

    def lru_scan(b):
        ctx[b]["h_lru"], carry["h"] = _linear_scan(ctx[b]["a"], ctx[b]["u"], carry["h"])

    def lru_out(b):
        ctx[b]["out_a"] = (jax.nn.gelu(z_refs[b][:, 0:lw], approximate=True) * ctx[b]["h_lru"]).astype(BF16)

    def hg_decay(b):
        f = lb + (1.0 - lb) * jax.nn.sigmoid(z_refs[b][:, 2 * lw + hw:2 * lw + 2 * hw])
        logf = jnp.log(f)
        hi = logf.astype(BF16)
        lo = (logf - hi.astype(F32)).astype(BF16)
        ctx[b]["k"] = 1.0 - f
        ctx[b]["bcum"] = _dot(tri, hi) + _dot(tri, lo)
        ctx[b]["qf"] = _silu(z_refs[b][:, 2 * lw:2 * lw + hw])

    def hg_factors(b):
        bcum, qf, k = ctx[b]["bcum"], ctx[b]["qf"], ctx[b]["k"]
        qs, ks, qin, kd, dec = [], [], [], [], []
        for c in range(n_c):
            cr = slice(c * HG_CHUNK, (c + 1) * HG_CHUNK)
            bc = bcum[cr]
            bm = bc[HG_CHUNK // 2 - 1:HG_CHUNK // 2]
            bl = bc[HG_CHUNK - 1:HG_CHUNK]
            e_c = jnp.exp(bc - bm)
            qs_c = qf[cr] * e_c
            ks_c = k[cr] * (1.0 / e_c)
            qs.append(qs_c)
            ks.append(ks_c)
            qin.append((qs_c * jnp.exp(bm)).astype(BF16))
            kd.append((ks_c * jnp.exp(bl - bm)).astype(BF16))
            dec.append(jnp.exp(bl))
        ctx[b].update(qs=jnp.concatenate(qs, axis=0).astype(BF16), ks=jnp.concatenate(ks, axis=0).astype(BF16),
                      qin=qin, kd=kd, dec=dec,
                      vb=z_refs[b][:, 2 * lw + 2 * hw:2 * lw + 3 * hw].astype(BF16),
                      gg=_silu(z_refs[b][:, 2 * lw + 3 * hw:2 * lw + 4 * hw]), out_b=[])

    def hg_head(b, hd):
        c_ = ctx[b]
        cs = slice(hd * dk, (hd + 1) * dk)
        vb = c_["vb"]
        sc = _dot_nt(c_["qs"][:, cs], c_["ks"][:, cs])
        kv = [_dot_tn(vb[c * HG_CHUNK:(c + 1) * HG_CHUNK, cs], c_["kd"][c][:, cs]) for c in range(n_c)]
        st = carry["st"][hd]
        st_in = []
        for c in range(n_c):
            st_in.append(st.astype(BF16))
            st = st * c_["dec"][c][:, cs] + kv[c]
        carry["st"][hd] = st
        o = _dot(jnp.where(tri_mask, sc, 0.0).astype(BF16), vb[:, cs])
        o_inter = [_dot_nt(c_["qin"][c][:, cs], st_in[c]) for c in range(n_c)]
        o = _rms(o + jnp.concatenate(o_inter, axis=0), gn)
        c_["out_b"].append((o * c_["gg"][:, cs]).astype(BF16))

    def out_proj(b, j):
        if "mix" not in ctx[b]:
            ctx[b]["mix"] = jnp.concatenate([ctx[b]["out_a"]] + ctx[b]["out_b"], axis=1)
        rows = row_blocks[b]
        cols = slice(j * ow, (j + 1) * ow)
        o_ref[rows, cols] = x_ref[rows, cols] + _dot(ctx[b]["mix"], wout_ref[j])

    P = functools.partial
    projs = [projection(P(lambda i: x_ref[row_blocks[i], :], i), z_refs[i]) for i in range(1, n_blocks)]
    projs.append(projection(lambda: xnext_ref[...], z_refs[0]))
    assert n_proj == 6 and n_out == 2
    schedule = []
    for b in range(n_blocks):
        pp = projs[b]
        heads = [P(hg_head, b, hd) for hd in range(HG_HEADS)]
        outs = [P(out_proj, b, j) for j in range(n_out)]
        if b == 0:
            schedule += [P(lru_conv, 0), pp[0], P(lru_gates, 0), pp[1], P(lru_scan, 0), pp[2], P(lru_out, 0), pp[3],
                         P(hg_decay, 0), pp[4], P(hg_factors, 0), pp[5]]
        else:
            schedule += [P(lru_conv, b), prev_heads[0], prev_heads[1], P(lru_gates, b), prev_heads[2], prev_heads[3],
                         P(lru_scan, b), prev_outs[0], P(lru_out, b), prev_outs[1],
                         P(hg_decay, b), pp[0], P(hg_factors, b), pp[1]]
            if b + 1 < n_blocks:
                schedule += pp[2:]
            else:
                schedule += [heads[0], pp[2], heads[1], pp[3], heads[2], pp[4], heads[3], pp[5], outs[0], outs[1]]
        prev_heads, prev_outs = heads, outs
    for stage in schedule:
        stage()

    xtail_ref[...] = carry["x_tail"]
    hprev_ref[...] = jnp.broadcast_to(carry["h"], hprev_ref.shape)
    for hd in range(HG_HEADS):
        st_ref[hd] = carry["st"][hd]


def _block_diag(w):
    n, bi, bj = w.shape
    eye = jnp.eye(n, dtype=w.dtype)
    return (w[:, :, None, :] * eye[:, None, :, None]).reshape(n * bi, n * bj)


def _even_mixer(h, g, w_in, conv_w, conv_b, ga_w, ga_b, gx_w, gx_b, lam, lb_logits, hg_norm, w_out, *, layer):
    bsz, seq, d = h.shape
    lw = conv_w.shape[1]
    hw = lb_logits.shape[1]
    dk = hw // HG_HEADS
    blk = MXU_EDGE
    n_blocks = EVEN_BLOCKS_PER_TILE
    tt = n_blocks * blk
    assert lw % blk == 0 and seq % tt == 0 and hg_norm.shape[0] == dk
    assert w_in.shape[1] == 2 * lw + 4 * hw and w_out.shape[0] == lw + hw and lw == hw
    n_gb = lw // blk
    bda = _block_diag(ga_w)
    bdx = _block_diag(gx_w)
    wg = jnp.stack([
        jnp.concatenate([bda[i * blk:(i + 1) * blk, i * blk:(i + 1) * blk],
                         bdx[i * blk:(i + 1) * blk, i * blk:(i + 1) * blk]], axis=1)
        for i in range(n_gb)]).astype(BF16)
    win = _col_blocks(w_in, lw).astype(BF16)
    wout = _col_blocks(w_out, 2 * blk).astype(BF16)
    n_tiles = bsz * seq // tt
    x2 = h.reshape(bsz * seq, d)
    kern = functools.partial(_even_kernel, layer=layer, lw=lw, hw=hw, tiles_per_row=seq // tt, n_blocks=n_blocks)

    def cspec(shape):
        nd = len(shape)
        return pl.BlockSpec(shape, lambda s: (0,) * nd, pipeline_mode=pl.Buffered(1))

    out = pl.pallas_call(
        kern,
        grid=(n_tiles,),
        in_specs=[
            pl.BlockSpec((tt, d), lambda s: (s, 0)),
            pl.BlockSpec((blk, d), lambda s: (jnp.minimum(n_blocks * (s + 1), n_blocks * n_tiles - 1), 0)),
            cspec((1, d)),
            cspec(win.shape),
            cspec(conv_w.shape),
            cspec((1, lw)),
            cspec(wg.shape),
            cspec((1, lw)),
            cspec((1, lw)),
            cspec((1, lw)),
            cspec(lb_logits.shape),
            cspec((1, dk)),
            cspec(wout.shape),
        ],
        out_specs=pl.BlockSpec((tt, d), lambda s: (s, 0)),
        out_shape=jax.ShapeDtypeStruct(x2.shape, x2.dtype),
        scratch_shapes=[pltpu.VMEM((blk, w_in.shape[1]), F32)] * n_blocks + [
            pltpu.VMEM((HG_HEADS, dk, dk), F32),
            pltpu.VMEM((SUBLANES, lw), F32),
            pltpu.VMEM((SUBLANES, lw), F32),
        ],
        compiler_params=pltpu.CompilerParams(
            dimension_semantics=("arbitrary",), vmem_limit_bytes=VMEM_LIMIT_BYTES),
        name="lru_hgrn_mixer",
    )(x2, x2, g.reshape(1, d), win, conv_w, conv_b.reshape(1, lw), wg, ga_b.reshape(1, lw),
      gx_b.reshape(1, lw), lam.reshape(1, lw), lb_logits, hg_norm.reshape(1, dk), wout)
    return out.reshape(bsz, seq, d)


def kernel(x, norm_mix, norm_ffn, norm_final, ev_w_in, ev_conv_w, ev_conv_b, ev_gate_a_w, ev_gate_a_b, ev_gate_x_w, ev_gate_x_b, ev_lru_lambda, hg_lb_logits, ev_hg_norm, ev_w_out, od_w_in, od_b_in, od_ln_g, od_ln_b, od_w_s, od_b_s, od_w_out, ffn_w_up, ffn_conv_w, ffn_conv_b, ffn_w_down):
    depth = norm_mix.shape[0]
    tt = min(TIME_TILE, x.shape[1])
    wup_all = _ffn_up_blocks(ffn_w_up)
    wdn_all = ffn_w_down.astype(BF16)
    h = x
    for layer in range(depth):
        if layer % 2 == 0:
            e = layer // 2
            h = _even_mixer(h, norm_mix[layer], ev_w_in[e], ev_conv_w[e], ev_conv_b[e], ev_gate_a_w[e],
                            ev_gate_a_b[e], ev_gate_x_w[e], ev_gate_x_b[e], ev_lru_lambda[e], hg_lb_logits,
                            ev_hg_norm[e], ev_w_out[e], layer=layer)
        else:
            o = layer // 2
            h = _odd_mixer(h, norm_mix[layer], od_w_in[o], od_b_in[o], od_ln_g[o], od_ln_b[o], od_w_s[o],
                           od_b_s[o], od_w_out[o], tt=tt)
        h = _ffn(h, norm_ffn[layer], wup_all, ffn_conv_w[layer], ffn_conv_b[layer], wdn_all,
                 norm_final, layer=layer, final=(layer == depth - 1), tt=tt)
    return h
```
